```python
import jax, jax.numpy as jnp
from jax import lax
import numpy as np

D_MODEL = 1024
BATCH = 8
SEQ = 4096
DEPTH = 2

ATT_HEADS = 8
HEAD_DIM = 64
D_ATT = ATT_HEADS * HEAD_DIM
Q_BLOCK = 128
ATT_SCALE = HEAD_DIM ** -0.5
FORGET_BIAS_MEAN = 2.0
D_CONV = 256
CONV_WIDTH = 31
POOL_WINDOWS = (2, 4, 8, 16)
N_POOL = len(POOL_WINDOWS)
POOL_GROUP = 64
D_POOL = N_POOL * POOL_GROUP
D_MIX = D_ATT + D_CONV + D_POOL
D_IN = 3 * D_ATT + ATT_HEADS + 2 * D_CONV + D_POOL
D_FF = 2816
FFN_CONV_WIDTH = 3
EPS = 1e-6
NEG = -1e30

kernel_name = 'hybrid_fox_conformer_pool_block'


def rms_norm(x, g):
    xf = x.astype(jnp.float32)
    y = xf * lax.rsqrt(jnp.mean(xf * xf, axis=-1, keepdims=True) + EPS)
    return (y * g.astype(jnp.float32)).astype(x.dtype)


def layer_norm(x, g, b):
    xf = x.astype(jnp.float32)
    mu = jnp.mean(xf, axis=-1, keepdims=True)
    var = jnp.mean(jnp.square(xf - mu), axis=-1, keepdims=True)
    y = (xf - mu) * lax.rsqrt(var + EPS)
    return (y * g.astype(jnp.float32) + b.astype(jnp.float32)).astype(x.dtype)


def causal_depthwise_conv(x, w):
    width, channels = w.shape
    return lax.conv_general_dilated(
        x, w.astype(x.dtype)[:, None, :], window_strides=(1,),
        padding=[(width - 1, 0)], dimension_numbers=('NWC', 'WIO', 'NWC'),
        feature_group_count=channels)


def forgetting_attention(q, k, v, fg_logit):
    B, S, H, Dh = q.shape
    log_f = jax.nn.log_sigmoid(fg_logit.astype(jnp.float32))
    cum = jnp.transpose(jnp.cumsum(log_f, axis=1), (0, 2, 1))
    outs = []
    for start in range(0, S, Q_BLOCK):
        end = start + Q_BLOCK
        qb = q[:, start:end]
        kb = k[:, :end]
        vb = v[:, :end]
        s = jnp.einsum('bqhd,bkhd->bhqk', qb, kb).astype(jnp.float32) * ATT_SCALE
        s = s + cum[:, :, start:end, None] - cum[:, :, None, :end]
        qi = jnp.arange(start, end)[:, None]
        ki = jnp.arange(end)[None, :]
        s = jnp.where(ki <= qi, s, NEG)
        p = jax.nn.softmax(s, axis=-1).astype(vb.dtype)
        outs.append(jnp.einsum('bhqk,bkhd->bqhd', p, vb))
    return jnp.concatenate(outs, axis=1)


def conformer_conv(u, dw_w, dw_b, ln_g, ln_b, pw_w):
    a, b = jnp.split(u, 2, axis=-1)
    h = a * jax.nn.sigmoid(b)
    h = causal_depthwise_conv(h, dw_w) + dw_b.astype(h.dtype)
    h = layer_norm(h, ln_g, ln_b)
    h = jax.nn.silu(h)
    return h @ pw_w.astype(h.dtype)


def multiscale_pool(u, pool_w, pool_scale):
    B, S, _ = u.shape
    uf = u.astype(jnp.float32)
    cs = jnp.cumsum(uf, axis=1)
    pos = jnp.arange(1, S + 1, dtype=jnp.float32)[:, None]
    diffs = []
    for g, w in enumerate(POOL_WINDOWS):
        sl = slice(g * POOL_GROUP, (g + 1) * POOL_GROUP)
        c = cs[..., sl]
        prev = jnp.pad(c, ((0, 0), (w, 0), (0, 0)))[:, :S]
        mean = (c - prev) / jnp.minimum(pos, float(w))
        diffs.append(mean - uf[..., sl])
    d = jnp.stack(diffs, axis=2)
    y = jnp.einsum('bsgc,gcd->bsgd', d, pool_w.astype(jnp.float32))
    y = y.reshape(B, S, D_POOL) * pool_scale.astype(jnp.float32)
    return y.astype(u.dtype)


def setup_inputs(seed: int = 0) -> dict:
    key = jax.random.key(seed)
    ks = jax.random.split(key, 20)
    f32 = jnp.float32
    L = DEPTH

    def nrm(k, shape, scale):
        return jax.random.normal(k, shape, f32) * scale

    return {
        'x': nrm(ks[0], (BATCH, SEQ, D_MODEL), 1.0),
        'norm1_g': 1.0 + nrm(ks[1], (L, D_MODEL), 0.02),
        'w_in': nrm(ks[2], (L, D_MODEL, D_IN), D_MODEL ** -0.5),
        'b_f': FORGET_BIAS_MEAN + nrm(ks[3], (L, ATT_HEADS), 0.5),
        'q_norm_g': 1.0 + nrm(ks[4], (L, HEAD_DIM), 0.02),
        'k_norm_g': 1.0 + nrm(ks[5], (L, HEAD_DIM), 0.02),
        'conv_dw_w': nrm(ks[6], (L, CONV_WIDTH, D_CONV), CONV_WIDTH ** -0.5),
        'conv_dw_b': nrm(ks[7], (L, D_CONV), 0.02),
        'conv_ln_g': 1.0 + nrm(ks[8], (L, D_CONV), 0.02),
        'conv_ln_b': nrm(ks[9], (L, D_CONV), 0.02),
        'conv_pw_w': nrm(ks[10], (L, D_CONV, D_CONV), D_CONV ** -0.5),
        'pool_w': nrm(ks[11], (L, N_POOL, POOL_GROUP, POOL_GROUP), POOL_GROUP ** -0.5),
        'pool_scale': 1.0 + nrm(ks[12], (L, D_POOL), 0.1),
        'w_out': nrm(ks[13], (L, D_MIX, D_MODEL), D_MIX ** -0.5),
        'norm2_g': 1.0 + nrm(ks[14], (L, D_MODEL), 0.02),
        'w_up': nrm(ks[15], (L, D_MODEL, 2 * D_FF), D_MODEL ** -0.5),
        'ffn_dw_w': nrm(ks[16], (L, FFN_CONV_WIDTH, 2 * D_FF), FFN_CONV_WIDTH ** -0.5),
        'w_down': nrm(ks[17], (L, D_FF, D_MODEL), D_FF ** -0.5),
    }


def reference(x, norm1_g, w_in, b_f, q_norm_g, k_norm_g, conv_dw_w, conv_dw_b,
              conv_ln_g, conv_ln_b, conv_pw_w, pool_w, pool_scale, w_out,
              norm2_g, w_up, ffn_dw_w, w_down):
    B, S, _ = x.shape
    for l in range(DEPTH):
        h = rms_norm(x, norm1_g[l])
        proj = h @ w_in[l].astype(h.dtype)
        o = 0
        q = proj[..., o:o + D_ATT]; o += D_ATT
        k = proj[..., o:o + D_ATT]; o += D_ATT
        v = proj[..., o:o + D_ATT]; o += D_ATT
        fg = proj[..., o:o + ATT_HEADS]; o += ATT_HEADS
        conv_in = proj[..., o:o + 2 * D_CONV]; o += 2 * D_CONV
        pool_in = proj[..., o:o + D_POOL]

        q = rms_norm(q.reshape(B, S, ATT_HEADS, HEAD_DIM), q_norm_g[l])
        k = rms_norm(k.reshape(B, S, ATT_HEADS, HEAD_DIM), k_norm_g[l])
        v = v.reshape(B, S, ATT_HEADS, HEAD_DIM)
        fg = fg + b_f[l].astype(fg.dtype)
        att = forgetting_attention(q, k, v, fg).reshape(B, S, D_ATT)

        conv = conformer_conv(conv_in, conv_dw_w[l], conv_dw_b[l],
                              conv_ln_g[l], conv_ln_b[l], conv_pw_w[l])
        pool = multiscale_pool(pool_in, pool_w[l], pool_scale[l])

        mix = jnp.concatenate([att, conv, pool], axis=-1)
        x = x + mix @ w_out[l].astype(mix.dtype)

        h2 = rms_norm(x, norm2_g[l])
        up = h2 @ w_up[l].astype(h2.dtype)
        up = causal_depthwise_conv(up, ffn_dw_w[l])
        gate, val = jnp.split(up, 2, axis=-1)
        x = x + (jax.nn.silu(gate) * val) @ w_down[l].astype(up.dtype)
    return x
```

```python
import functools

import numpy as np
import jax
import jax.numpy as jnp
from jax import lax
from jax.experimental import pallas as pl
from jax.experimental.pallas import tpu as pltpu

D_MODEL = 1024
ATT_HEADS = 8
HEAD_DIM = 64
D_ATT = ATT_HEADS * HEAD_DIM
N_PAIRS = ATT_HEADS // 2
ATT_SCALE = HEAD_DIM ** -0.5
D_CONV = 256
CONV_WIDTH = 31
POOL_WINDOWS = (2, 4, 8, 16)
POOL_GROUP = 64
D_POOL = len(POOL_WINDOWS) * POOL_GROUP
D_FF = 2816
FFN_CONV_WIDTH = 3
EPS = 1e-6
NEG = -1e30
LOG2E = 1.4426950408889634

LANES = 128
SUBLANES = 8
FG_LANES = LANES
N_PROJ = 3 * D_ATT + 2 * D_CONV + D_POOL + FG_LANES
ROW_TILE = 512
HALO = 32
FF_CHUNK = 256
N_FF_CHUNKS = D_FF // FF_CHUNK
VMEM_LIMIT = 56 * 1024 * 1024

f32 = jnp.float32
bf16 = jnp.bfloat16

_SPLITS = 3


def _placement_matrices():
    eq = np.zeros((LANES, N_PAIRS * LANES), np.float32)
    ek = np.zeros((LANES, N_PAIRS * LANES), np.float32)
    for h in range(ATT_HEADS):
        p, hh = divmod(h, 2)
        for s in range(_SPLITS):
            src = s * ATT_HEADS + h
            ek[src, p * LANES + hh * _SPLITS + s] = -1.0
            eq[src, p * LANES + 2 * _SPLITS + hh * _SPLITS + s] = 1.0
    one_lane = _SPLITS * ATT_HEADS
    for p in range(N_PAIRS):
        ek[one_lane, p * LANES + 2 * _SPLITS:p * LANES + 4 * _SPLITS] = 1.0
        eq[one_lane, p * LANES:p * LANES + 2 * _SPLITS] = 1.0
    return eq, ek


def _split3(c):
    hi = c.astype(bf16).astype(f32)
    r1 = c - hi
    mid = r1.astype(bf16).astype(f32)
    lo = r1 - mid
    return hi, mid, lo


def _pack_groups(lane, a0, a1, a2, rest):
    return jnp.where(lane < ATT_HEADS, a0,
                     jnp.where(lane < 2 * ATT_HEADS, a1,
                               jnp.where(lane < 3 * ATT_HEADS, a2, rest)))


def _in_kernel(x_ref, g1_ref, w_ref, bf_ref, gq_ref, gk_ref, grp_ref, tri_ref, eq_ref, ek_ref,
               qq_ref, kk_ref, vt_ref, conv_ref, pool_ref, carry_ref, *, tiles_per_seq):
    tm = x_ref.shape[0]

    @pl.when(pl.program_id(0) % tiles_per_seq == 0)
    def _():
        carry_ref[...] = jnp.zeros_like(carry_ref)

    x = x_ref[...]
    ms = jnp.mean(x * x, axis=-1, keepdims=True)
    h = (x * lax.rsqrt(ms + EPS) * g1_ref[...]).astype(bf16)
    proj = jnp.dot(h, w_ref[...], preferred_element_type=f32)
    q = proj[:, 0:D_ATT]
    k = proj[:, D_ATT:2 * D_ATT]
    v = proj[:, 2 * D_ATT:3 * D_ATT]
    o = 3 * D_ATT
    conv_ref[...] = proj[:, o:o + 2 * D_CONV]
    o += 2 * D_CONV
    pool_ref[...] = proj[:, o:o + D_POOL]
    o += D_POOL
    fg = proj[:, o:o + FG_LANES]

    grp = grp_ref[...]

    def head_norm(t, g_ref, mult):
        tt = t * t
        hi = tt.astype(bf16)
        lo = (tt - hi.astype(f32)).astype(bf16)
        halves = []
        for c in range(D_ATT // 256):
            sl = slice(c * 256, (c + 1) * 256)
            halves.append(jnp.dot(hi[:, sl], grp, preferred_element_type=f32)
                          + jnp.dot(lo[:, sl], grp, preferred_element_type=f32))
        msq = jnp.concatenate(halves, axis=1)
        y = t * lax.rsqrt(msq + EPS) * g_ref[...]
        if mult != 1.0:
            y = y * mult
        return y.astype(bf16)

    qn = head_norm(q, gq_ref, ATT_SCALE * LOG2E)
    kn = head_norm(k, gk_ref, 1.0)

    lane = lax.broadcasted_iota(jnp.int32, (tm, FG_LANES), 1)
    lf = jax.nn.log_sigmoid(fg + bf_ref[...]) * LOG2E
    l0, l1, l2 = _split3(lf)
    packed = _pack_groups(lane, l0, l1, l2, 0.0).astype(bf16)
    part = jnp.dot(tri_ref[...], packed, preferred_element_type=f32)
    g = ATT_HEADS
    tot = (part + pltpu.roll(part, g, 1) + pltpu.roll(part, 2 * g, 1)
           + pltpu.roll(part, FG_LANES - g, 1) + pltpu.roll(part, FG_LANES - 2 * g, 1))
    cum = tot + carry_ref[0:1, :]
    carry_ref[...] = jnp.broadcast_to(cum[tm - 1:tm, :], carry_ref.shape)

    c0, c1, c2 = _split3(cum)
    ones_lane = jnp.where(lane == _SPLITS * ATT_HEADS, 1.0, 0.0)
    aug_src = _pack_groups(lane, c0, c1, c2, ones_lane).astype(bf16)
    qa = jnp.dot(aug_src, eq_ref[...], preferred_element_type=f32).astype(bf16)
    ka = jnp.dot(aug_src, ek_ref[...], preferred_element_type=f32).astype(bf16)

    def interleave(a, b):
        pieces = []
        for p in range(N_PAIRS):
            sl = slice(p * LANES, (p + 1) * LANES)
            pieces += [a[:, sl], b[:, sl]]
        return jnp.concatenate(pieces, axis=1)

    qq_ref[...] = interleave(qn, qa)
    kk_ref[...] = interleave(kn, ka)

    vt = v.T.astype(bf16)
    for p in range(N_PAIRS):
        vt_ref[0, p, 0] = vt[p * LANES:(p + 1) * LANES, :]


def _in_proj(x2, g1, w_cat, bf_pad, gq, gk, consts, batch, seq):
    t = x2.shape[0]
    tm = ROW_TILE
    tiles_per_seq = seq // tm
    n_tiles = t // tm
    grp, tri, eq, ek = consts
    const = lambda shape: pl.BlockSpec(shape, lambda i: (0,) * len(shape))
    return pl.pallas_call(
        functools.partial(_in_kernel, tiles_per_seq=tiles_per_seq),
        grid=(n_tiles,),
        in_specs=[
            pl.BlockSpec((tm, D_MODEL), lambda i: (i, 0)),
            const((1, D_MODEL)),
            const((D_MODEL, N_PROJ)),
            const((1, FG_LANES)),
            const((1, D_ATT)),
            const((1, D_ATT)),
            const((256, 256)),
            const((tm, tm)),
            const((LANES, N_PAIRS * LANES)),
            const((LANES, N_PAIRS * LANES)),
        ],
        out_specs=[
            pl.BlockSpec((tm, 2 * D_ATT), lambda i: (i, 0)),
            pl.BlockSpec((tm, 2 * D_ATT), lambda i: (i, 0)),
            pl.BlockSpec((1, N_PAIRS, 1, LANES, tm),
                         lambda i: (i // tiles_per_seq, 0, i % tiles_per_seq, 0, 0)),
            pl.BlockSpec((tm, 2 * D_CONV), lambda i: (i, 0)),
            pl.BlockSpec((tm, D_POOL), lambda i: (i, 0)),
        ],
        out_shape=[
            jax.ShapeDtypeStruct((t, 2 * D_ATT), bf16),
            jax.ShapeDtypeStruct((t, 2 * D_ATT), bf16),
            jax.ShapeDtypeStruct((batch, N_PAIRS, tiles_per_seq, LANES, tm), bf16),
            jax.ShapeDtypeStruct((t, 2 * D_CONV), f32),
            jax.ShapeDtypeStruct((t, D_POOL), f32),
        ],
        scratch_shapes=[pltpu.VMEM((SUBLANES, FG_LANES), f32)],
        compiler_params=pltpu.CompilerParams(
            dimension_semantics=("arbitrary",), vmem_limit_bytes=VMEM_LIMIT),
        name="in_proj",
    )(x2, g1, w_cat, bf_pad, gq, gk, grp, tri, eq, ek)


def _mix_kernel(cv_ref, pin_ref, dww_ref, dwb_ref, lng_ref, lnb_ref, pw_ref, poolw_ref,
                pscale_ref, conv_out_ref, pool_out_ref, hbuf, ubuf, *, tiles_per_seq):
    tm = cv_ref.shape[0]
    t_in_seq = pl.program_id(0) % tiles_per_seq

    @pl.when(t_in_seq == 0)
    def _():
        hbuf[0:HALO, :] = jnp.zeros((HALO, D_CONV), f32)
        ubuf[0:HALO, :] = jnp.zeros((HALO, D_POOL), f32)

    cv = cv_ref[...]
    hbuf[HALO:HALO + tm, :] = cv[:, :D_CONV] * jax.nn.sigmoid(cv[:, D_CONV:])
    acc = jnp.broadcast_to(dwb_ref[...], (tm, D_CONV))
    base = HALO - (CONV_WIDTH - 1)
    for kk in range(CONV_WIDTH):
        acc = acc + dww_ref[kk:kk + 1, :] * hbuf[base + kk:base + kk + tm, :]
    hbuf[0:HALO, :] = hbuf[tm:tm + HALO, :]
    mu = jnp.mean(acc, axis=-1, keepdims=True)
    cen = acc - mu
    var = jnp.mean(cen * cen, axis=-1, keepdims=True)
    y = cen * lax.rsqrt(var + EPS) * lng_ref[...] + lnb_ref[...]
    y = y * jax.nn.sigmoid(y)
    conv_out_ref[...] = jnp.dot(y.astype(bf16), pw_ref[...],
                                preferred_element_type=f32).astype(bf16)

    u = pin_ref[...]
    ubuf[HALO:HALO + tm, :] = u
    sums = {}
    run = u
    for j in range(1, max(POOL_WINDOWS)):
        run = run + ubuf[HALO - j:HALO - j + tm, :]
        if j + 1 in POOL_WINDOWS:
            sums[j + 1] = run
    ubuf[0:HALO, :] = ubuf[tm:tm + HALO, :]
    lane = lax.broadcasted_iota(jnp.int32, (tm, D_POOL), 1)
    row = lax.broadcasted_iota(jnp.int32, (tm, D_POOL), 0)
    pos = (row + t_in_seq * tm + 1).astype(f32)
    box = sums[POOL_WINDOWS[-1]]
    win = jnp.full((tm, D_POOL), float(POOL_WINDOWS[-1]), f32)
    for gi in range(len(POOL_WINDOWS) - 2, -1, -1):
        sel = lane < (gi + 1) * POOL_GROUP
        box = jnp.where(sel, sums[POOL_WINDOWS[gi]], box)
        win = jnp.where(sel, float(POOL_WINDOWS[gi]), win)
    d = box / jnp.minimum(pos, win) - u
    yp = jnp.dot(d.astype(bf16), poolw_ref[...], preferred_element_type=f32)
    pool_out_ref[...] = (yp * pscale_ref[...]).astype(bf16)


def _mixers(cv, pin, dww, dwb, lng, lnb, pw, poolw, pscale, seq):
    t = cv.shape[0]
    tm = ROW_TILE
    tiles_per_seq = seq // tm
    const = lambda shape: pl.BlockSpec(shape, lambda i: (0,) * len(shape))
    return pl.pallas_call(
        functools.partial(_mix_kernel, tiles_per_seq=tiles_per_seq),
        grid=(t // tm,),
        in_specs=[
            pl.BlockSpec((tm, 2 * D_CONV), lambda i: (i, 0)),
            pl.BlockSpec((tm, D_POOL), lambda i: (i, 0)),
            const((HALO, D_CONV)),
            const((1, D_CONV)),
            const((1, D_CONV)),
            const((1, D_CONV)),
            const((D_CONV, D_CONV)),
            const((D_POOL, D_POOL)),
            const((1, D_POOL)),
        ],
        out_specs=[
            pl.BlockSpec((tm, D_CONV), lambda i: (i, 0)),
            pl.BlockSpec((tm, D_POOL), lambda i: (i, 0)),
        ],
        out_shape=[
            jax.ShapeDtypeStruct((t, D_CONV), bf16),
            jax.ShapeDtypeStruct((t, D_POOL), bf16),
        ],
        scratch_shapes=[pltpu.VMEM((HALO + tm, D_CONV), f32),
                        pltpu.VMEM((HALO + tm, D_POOL), f32)],
        compiler_params=pltpu.CompilerParams(
            dimension_semantics=("arbitrary",), vmem_limit_bytes=VMEM_LIMIT),
        name="mixers",
    )(cv, pin, dww, dwb, lng, lnb, pw, poolw, pscale)


def _att_kernel(qq_ref, kk_ref, vt_ref, o_ref):
    tq = qq_ref.shape[1]
    i = pl.program_id(2)
    qq = qq_ref[0]
    lane = lax.broadcasted_iota(jnp.int32, (1, 2 * LANES), 1)
    aug = lane - LANES
    row = lax.broadcasted_iota(jnp.int32, (LANES, 1), 0)

    def in_range(v, lo, hi):
        return (v >= lo) & (v < hi)

    s = _SPLITS
    keep = [
        in_range(lane, 0, HEAD_DIM) | in_range(aug, 0, s) | in_range(aug, 2 * s, 3 * s),
        in_range(lane, HEAD_DIM, LANES) | in_range(aug, s, 2 * s) | in_range(aug, 3 * s, 4 * s),
    ]
    qh = [jnp.where(kp, qq, jnp.zeros_like(qq)) for kp in keep]
    vkeep = [row < HEAD_DIM, row >= HEAD_DIM]

    def step(j, carry, masked):
        off = pl.multiple_of(j * tq, tq)
        kt = kk_ref[0, pl.ds(off, tq), :]
        vt = vt_ref[0, 0, j]
        new = []
        for hh in range(2):
            m, acc = carry[2 * hh], carry[2 * hh + 1]
            st = lax.dot_general(kt, qh[hh], (((1,), (1,)), ((), ())),
                                 preferred_element_type=f32)
            if masked:
                kidx = lax.broadcasted_iota(jnp.int32, st.shape, 0)
                qidx = lax.broadcasted_iota(jnp.int32, st.shape, 1)
                st = jnp.where(kidx <= qidx, st, NEG)
            m_new = jnp.maximum(m, jnp.max(st, axis=0, keepdims=True))
            p = jnp.exp2(st - m_new).astype(bf16)
            alpha = jnp.exp2(m - m_new)
            vh = jnp.where(vkeep[hh], vt, jnp.ones_like(vt))
            acc = acc * alpha + jnp.dot(vh, p, preferred_element_type=f32)
            new += [m_new, acc]
        return tuple(new)

    m0 = jnp.full((1, tq), NEG, f32)
    a0 = jnp.zeros((LANES, tq), f32)
    carry = lax.fori_loop(0, i, lambda j, c: step(j, c, False), (m0, a0, m0, a0))
    _, acc0, _, acc1 = step(i, carry, True)
    out0 = acc0 / acc0[HEAD_DIM:HEAD_DIM + 1, :]
    out1 = acc1 / acc1[0:1, :]
    ot = jnp.where(row < HEAD_DIM, out0, out1)
    o_ref[0] = ot.T.astype(bf16)


def _attention(qq, kk, vt, batch, seq):
    tq = ROW_TILE
    nq = seq // tq
    return pl.pallas_call(
        _att_kernel,
        grid=(batch, N_PAIRS, nq),
        in_specs=[
            pl.BlockSpec((1, tq, 2 * LANES), lambda b, p, i: (b, i, p)),
            pl.BlockSpec((1, seq, 2 * LANES), lambda b, p, i: (b, 0, p)),
            pl.BlockSpec((1, 1, nq, LANES, tq), lambda b, p, i: (b, p, 0, 0, 0)),
        ],
        out_specs=pl.BlockSpec((1, tq, LANES), lambda b, p, i: (b, i, p)),
        out_shape=jax.ShapeDtypeStruct((batch, seq, D_ATT), bf16),
        compiler_params=pltpu.CompilerParams(
            dimension_semantics=("arbitrary", "arbitrary", "arbitrary"),
            vmem_limit_bytes=VMEM_LIMIT),
        name="fox_attention",
    )(qq, kk, vt)


def _ffn_kernel(x_ref, att_ref, conv_ref, pool_ref, wo_ref, g2_ref, wup_ref, dw_ref, wdn_ref,
                out_ref, tail_ref, act_ref, *, tiles_per_seq):
    tm = x_ref.shape[0]

    @pl.when(pl.program_id(0) % tiles_per_seq == 0)
    def _():
        tail_ref[...] = jnp.zeros_like(tail_ref)

    mix = jnp.concatenate([att_ref[...], conv_ref[...], pool_ref[...]], axis=1)
    x1 = x_ref[...] + jnp.dot(mix, wo_ref[...], preferred_element_type=f32)
    ms = jnp.mean(x1 * x1, axis=-1, keepdims=True)
    h2 = (x1 * lax.rsqrt(ms + EPS) * g2_ref[...]).astype(bf16)

    for c in range(N_FF_CHUNKS):
        up = jnp.dot(h2, wup_ref[c], preferred_element_type=f32)
        ext = jnp.concatenate([tail_ref[c], up], axis=0)
        tail_ref[c] = up[tm - SUBLANES:tm, :]
        s1 = pltpu.roll(ext, 1, 0)[SUBLANES:, :]
        s2 = pltpu.roll(ext, 2, 0)[SUBLANES:, :]
        w = dw_ref[c]
        cvd = w[0:1, :] * s2 + w[1:2, :] * s1 + w[2:3, :] * up
        gate = cvd[:, :FF_CHUNK]
        val = cvd[:, FF_CHUNK:]
        act_ref[:, c * FF_CHUNK:(c + 1) * FF_CHUNK] = (
            gate * jax.nn.sigmoid(gate) * val).astype(bf16)

    out_ref[...] = x1 + jnp.dot(act_ref[...], wdn_ref[...], preferred_element_type=f32)


def _ffn(x2, att, conv, pool, wo, g2, wup, dw, wdn, seq):
    t = x2.shape[0]
    tm = ROW_TILE
    tiles_per_seq = seq // tm
    resident = lambda shape: pl.BlockSpec(shape, lambda i: (0,) * len(shape),
                                          pipeline_mode=pl.Buffered(1))
    return pl.pallas_call(
        functools.partial(_ffn_kernel, tiles_per_seq=tiles_per_seq),
        grid=(t // tm,),
        in_specs=[
            pl.BlockSpec((tm, D_MODEL), lambda i: (i, 0)),
            pl.BlockSpec((tm, D_ATT), lambda i: (i, 0)),
            pl.BlockSpec((tm, D_CONV), lambda i: (i, 0)),
            pl.BlockSpec((tm, D_POOL), lambda i: (i, 0)),
            resident((D_MODEL, D_MODEL)),
            resident((1, D_MODEL)),
            resident((N_FF_CHUNKS, D_MODEL, 2 * FF_CHUNK)),
            resident((N_FF_CHUNKS, SUBLANES, 2 * FF_CHUNK)),
            resident((D_FF, D_MODEL)),
        ],
        out_specs=pl.BlockSpec((tm, D_MODEL), lambda i: (i, 0)),
        out_shape=jax.ShapeDtypeStruct((t, D_MODEL), f32),
        scratch_shapes=[pltpu.VMEM((N_FF_CHUNKS, SUBLANES, 2 * FF_CHUNK), f32),
                        pltpu.VMEM((tm, D_FF), bf16)],
        compiler_params=pltpu.CompilerParams(
            dimension_semantics=("arbitrary",), vmem_limit_bytes=VMEM_LIMIT),
        name="out_proj_ffn",
    )(x2, att, conv, pool, wo, g2, wup, dw, wdn)


def _constants():
    grp = np.kron(np.eye(256 // HEAD_DIM, dtype=np.float32),
                  np.full((HEAD_DIM, HEAD_DIM), 1.0 / HEAD_DIM, np.float32))
    tri = np.tril(np.ones((ROW_TILE, ROW_TILE), np.float32))
    eq, ek = _placement_matrices()
    return tuple(jnp.asarray(a, bf16) for a in (grp, tri, eq, ek))


def _chunk_ff(w):
    lead = w.shape[:-1]
    gate = w[..., :D_FF].reshape(lead + (N_FF_CHUNKS, FF_CHUNK))
    val = w[..., D_FF:].reshape(lead + (N_FF_CHUNKS, FF_CHUNK))
    both = jnp.concatenate([gate, val], axis=-1)
    return jnp.moveaxis(both, -2, 0)


def kernel(x, norm1_g, w_in, b_f, q_norm_g, k_norm_g, conv_dw_w, conv_dw_b, conv_ln_g, conv_ln_b,
           conv_pw_w, pool_w, pool_scale, w_out, norm2_g, w_up, ffn_dw_w, w_down):
    batch, seq, _ = x.shape
    depth = w_in.shape[0]
    assert seq % ROW_TILE == 0
    consts = _constants()
    x2 = x.reshape(batch * seq, D_MODEL)

    for l in range(depth):
        wi = w_in[l]
        o = 3 * D_ATT
        w_fg = wi[:, o:o + ATT_HEADS]
        o += ATT_HEADS
        w_cv = wi[:, o:o + 2 * D_CONV]
        o += 2 * D_CONV
        w_pl = wi[:, o:o + D_POOL]
        fg_pad = jnp.zeros((D_MODEL, FG_LANES - _SPLITS * ATT_HEADS), f32)
        w_cat = jnp.concatenate([wi[:, :3 * D_ATT], w_cv, w_pl] + [w_fg] * _SPLITS + [fg_pad],
                                axis=1).astype(bf16)
        bf_pad = jnp.concatenate([b_f[l]] * _SPLITS
                                 + [jnp.zeros((FG_LANES - _SPLITS * ATT_HEADS,), f32)])[None, :]
        gq = jnp.tile(q_norm_g[l], ATT_HEADS)[None, :]
        gk = jnp.tile(k_norm_g[l], ATT_HEADS)[None, :]

        qq, kk, vt, cv, pin = _in_proj(x2, norm1_g[l][None, :], w_cat, bf_pad, gq, gk, consts,
                                       batch, seq)

        dww = jnp.concatenate([conv_dw_w[l], jnp.zeros((HALO - CONV_WIDTH, D_CONV), f32)], axis=0)
        poolw = jax.scipy.linalg.block_diag(*[pool_w[l, g] for g in range(len(POOL_WINDOWS))])
        conv, pool = _mixers(cv, pin, dww, conv_dw_b[l][None, :], conv_ln_g[l][None, :],
                             conv_ln_b[l][None, :], conv_pw_w[l].astype(bf16),
                             poolw.astype(bf16), pool_scale[l][None, :], seq)

        att = _attention(qq.reshape(batch, seq, 2 * D_ATT), kk.reshape(batch, seq, 2 * D_ATT),
                         vt, batch, seq).reshape(batch * seq, D_ATT)

        wup = _chunk_ff(w_up[l]).astype(bf16)
        dw = _chunk_ff(ffn_dw_w[l])
        dw = jnp.concatenate(
            [dw, jnp.zeros((N_FF_CHUNKS, SUBLANES - FFN_CONV_WIDTH, 2 * FF_CHUNK), f32)], axis=1)
        x2 = _ffn(x2, att, conv, pool, w_out[l].astype(bf16), norm2_g[l][None, :], wup, dw,
                  w_down[l].astype(bf16), seq)

    return x2.reshape(batch, seq, D_MODEL)
```

```python
import functools

import numpy as np
import jax
import jax.numpy as jnp
from jax import lax
from jax.experimental import pallas as pl
from jax.experimental.pallas import tpu as pltpu

D_MODEL = 1024
ATT_HEADS = 8
HEAD_DIM = 64
D_ATT = ATT_HEADS * HEAD_DIM
N_PAIRS = ATT_HEADS // 2
ATT_SCALE = HEAD_DIM ** -0.5
D_CONV = 256
CONV_WIDTH = 31
POOL_WINDOWS = (2, 4, 8, 16)
POOL_GROUP = 64
D_POOL = len(POOL_WINDOWS) * POOL_GROUP
D_FF = 2816
FFN_CONV_WIDTH = 3
EPS = 1e-6
NEG = -1e30
LOG2E = 1.4426950408889634

LANES = 128
SUBLANES = 8
FG_LANES = LANES
N_PROJ = 3 * D_ATT + 2 * D_CONV + D_POOL + FG_LANES
ROW_TILE = 512
HALO = 32
FF_CHUNK = 256
N_FF_CHUNKS = D_FF // FF_CHUNK
VMEM_LIMIT = 56 * 1024 * 1024

f32 = jnp.float32
bf16 = jnp.bfloat16

_SPLITS = 3


def _placement_matrices():
    eq = np.zeros((LANES, N_PAIRS * LANES), np.float32)
    ek = np.zeros((LANES, N_PAIRS * LANES), np.float32)
    for h in range(ATT_HEADS):
        p, hh = divmod(h, 2)
        for s in range(_SPLITS):
            src = s * ATT_HEADS + h
            ek[src, p * LANES + hh * _SPLITS + s] = -1.0
            eq[src, p * LANES + 2 * _SPLITS + hh * _SPLITS + s] = 1.0
    one_lane = _SPLITS * ATT_HEADS
    for p in range(N_PAIRS):
        ek[one_lane, p * LANES + 2 * _SPLITS:p * LANES + 4 * _SPLITS] = 1.0
        eq[one_lane, p * LANES:p * LANES + 2 * _SPLITS] = 1.0
    return eq, ek


def _split3(c):
    hi = c.astype(bf16).astype(f32)
    r1 = c - hi
    mid = r1.astype(bf16).astype(f32)
    lo = r1 - mid
    return hi, mid, lo


def _pack_groups(lane, a0, a1, a2, rest):
    return jnp.where(lane < ATT_HEADS, a0,
                     jnp.where(lane < 2 * ATT_HEADS, a1,
                               jnp.where(lane < 3 * ATT_HEADS, a2, rest)))


def _in_kernel(x_ref, g1_ref, w_ref, bf_ref, gq_ref, gk_ref, grp_ref, tri_ref, eq_ref, ek_ref,
               qq_ref, kk_ref, vt_ref, conv_ref, pool_ref, carry_ref, *, tiles_per_seq):
    tm = x_ref.shape[0]

    @pl.when(pl.program_id(0) % tiles_per_seq == 0)
    def _():
        carry_ref[...] = jnp.zeros_like(carry_ref)

    x = x_ref[...]
    ms = jnp.mean(x * x, axis=-1, keepdims=True)
    h = (x * lax.rsqrt(ms + EPS) * g1_ref[...]).astype(bf16)
    proj = jnp.dot(h, w_ref[...], preferred_element_type=f32)
    q = proj[:, 0:D_ATT]
    k = proj[:, D_ATT:2 * D_ATT]
    v = proj[:, 2 * D_ATT:3 * D_ATT]
    o = 3 * D_ATT
    conv_ref[...] = proj[:, o:o + 2 * D_CONV]
    o += 2 * D_CONV
    pool_ref[...] = proj[:, o:o + D_POOL]
    o += D_POOL
    fg = proj[:, o:o + FG_LANES]

    grp = grp_ref[...]

    def head_norm(t, g_ref, mult):
        tt = t * t
        hi = tt.astype(bf16)
        lo = (tt - hi.astype(f32)).astype(bf16)
        halves = []
        for c in range(D_ATT // 256):
            sl = slice(c * 256, (c + 1) * 256)
            halves.append(jnp.dot(hi[:, sl], grp, preferred_element_type=f32)
                          + jnp.dot(lo[:, sl], grp, preferred_element_type=f32))
        msq = jnp.concatenate(halves, axis=1)
        y = t * lax.rsqrt(msq + EPS) * g_ref[...]
        if mult != 1.0:
            y = y * mult
        return y.astype(bf16)

    qn = head_norm(q, gq_ref, ATT_SCALE * LOG2E)
    kn = head_norm(k, gk_ref, 1.0)

    lane = lax.broadcasted_iota(jnp.int32, (tm, FG_LANES), 1)
    lf = jax.nn.log_sigmoid(fg + bf_ref[...]) * LOG2E
    l0, l1, l2 = _split3(lf)
    packed = _pack_groups(lane, l0, l1, l2, 0.0).astype(bf16)
    part = jnp.dot(tri_ref[...], packed, preferred_element_type=f32)
    g = ATT_HEADS
    tot = (part + pltpu.roll(part, g, 1) + pltpu.roll(part, 2 * g, 1)
           + pltpu.roll(part, FG_LANES - g, 1) + pltpu.roll(part, FG_LANES - 2 * g, 1))
    cum = tot + carry_ref[0:1, :]
    carry_ref[...] = jnp.broadcast_to(cum[tm - 1:tm, :], carry_ref.shape)

    c0, c1, c2 = _split3(cum)
    ones_lane = jnp.where(lane == _SPLITS * ATT_HEADS, 1.0, 0.0)
    aug_src = _pack_groups(lane, c0, c1, c2, ones_lane).astype(bf16)
    qa = jnp.dot(aug_src, eq_ref[...], preferred_element_type=f32).astype(bf16)
    ka = jnp.dot(aug_src, ek_ref[...], preferred_element_type=f32).astype(bf16)

    def interleave(a, b):
        pieces = []
        for p in range(N_PAIRS):
            sl = slice(p * LANES, (p + 1) * LANES)
            pieces += [a[:, sl], b[:, sl]]
        return jnp.concatenate(pieces, axis=1)

    qq_ref[...] = interleave(qn, qa)
    kk_ref[...] = interleave(kn, ka)

    vt = v.T.astype(bf16)
    for p in range(N_PAIRS):
        vt_ref[0, p, 0] = vt[p * LANES:(p + 1) * LANES, :]


def _in_proj(x2, g1, w_cat, bf_pad, gq, gk, consts, batch, seq):
    t = x2.shape[0]
    tm = ROW_TILE
    tiles_per_seq = seq // tm
    n_tiles = t // tm
    grp, tri, eq, ek = consts
    const = lambda shape: pl.BlockSpec(shape, lambda i: (0,) * len(shape))
    return pl.pallas_call(
        functools.partial(_in_kernel, tiles_per_seq=tiles_per_seq),
        grid=(n_tiles,),
        in_specs=[
            pl.BlockSpec((tm, D_MODEL), lambda i: (i, 0)),
            const((1, D_MODEL)),
            const((D_MODEL, N_PROJ)),
            const((1, FG_LANES)),
            const((1, D_ATT)),
            const((1, D_ATT)),
            const((256, 256)),
            const((tm, tm)),
            const((LANES, N_PAIRS * LANES)),
            const((LANES, N_PAIRS * LANES)),
        ],
        out_specs=[
            pl.BlockSpec((tm, 2 * D_ATT), lambda i: (i, 0)),
            pl.BlockSpec((tm, 2 * D_ATT), lambda i: (i, 0)),
            pl.BlockSpec((1, N_PAIRS, 1, LANES, tm),
                         lambda i: (i // tiles_per_seq, 0, i % tiles_per_seq, 0, 0)),
            pl.BlockSpec((tm, 2 * D_CONV), lambda i: (i, 0)),
            pl.BlockSpec((tm, D_POOL), lambda i: (i, 0)),
        ],
        out_shape=[
            jax.ShapeDtypeStruct((t, 2 * D_ATT), bf16),
            jax.ShapeDtypeStruct((t, 2 * D_ATT), bf16),
            jax.ShapeDtypeStruct((batch, N_PAIRS, tiles_per_seq, LANES, tm), bf16),
            jax.ShapeDtypeStruct((t, 2 * D_CONV), f32),
            jax.ShapeDtypeStruct((t, D_POOL), f32),
        ],
        scratch_shapes=[pltpu.VMEM((SUBLANES, FG_LANES), f32)],
        compiler_params=pltpu.CompilerParams(
            dimension_semantics=("arbitrary",), vmem_limit_bytes=VMEM_LIMIT),
        name="in_proj",
    )(x2, g1, w_cat, bf_pad, gq, gk, grp, tri, eq, ek)


def _mix_kernel(cv_ref, pin_ref, dww_ref, dwb_ref, lng_ref, lnb_ref, pw_ref, poolw_ref,
                pscale_ref, conv_out_ref, pool_out_ref, hbuf, ubuf, *, tiles_per_seq):
    tm = cv_ref.shape[0]
    t_in_seq = pl.program_id(0) % tiles_per_seq

    @pl.when(t_in_seq == 0)
    def _():
        hbuf[0:HALO, :] = jnp.zeros((HALO, D_CONV), f32)
        ubuf[0:HALO, :] = jnp.zeros((HALO, D_POOL), f32)

    cv = cv_ref[...]
    hbuf[HALO:HALO + tm, :] = cv[:, :D_CONV] * jax.nn.sigmoid(cv[:, D_CONV:])
    acc = jnp.broadcast_to(dwb_ref[...], (tm, D_CONV))
    base = HALO - (CONV_WIDTH - 1)
    for kk in range(CONV_WIDTH):
        acc = acc + dww_ref[kk:kk + 1, :] * hbuf[base + kk:base + kk + tm, :]
    hbuf[0:HALO, :] = hbuf[tm:tm + HALO, :]
    mu = jnp.mean(acc, axis=-1, keepdims=True)
    cen = acc - mu
    var = jnp.mean(cen * cen, axis=-1, keepdims=True)
    y = cen * lax.rsqrt(var + EPS) * lng_ref[...] + lnb_ref[...]
    y = y * jax.nn.sigmoid(y)
    conv_out_ref[...] = jnp.dot(y.astype(bf16), pw_ref[...],
                                preferred_element_type=f32).astype(bf16)

    u = pin_ref[...]
    ubuf[HALO:HALO + tm, :] = u
    sums = {}
    run = u
    for j in range(1, max(POOL_WINDOWS)):
        run = run + ubuf[HALO - j:HALO - j + tm, :]
        if j + 1 in POOL_WINDOWS:
            sums[j + 1] = run
    ubuf[0:HALO, :] = ubuf[tm:tm + HALO, :]
    lane = lax.broadcasted_iota(jnp.int32, (tm, D_POOL), 1)
    row = lax.broadcasted_iota(jnp.int32, (tm, D_POOL), 0)
    pos = (row + t_in_seq * tm + 1).astype(f32)
    box = sums[POOL_WINDOWS[-1]]
    win = jnp.full((tm, D_POOL), float(POOL_WINDOWS[-1]), f32)
    for gi in range(len(POOL_WINDOWS) - 2, -1, -1):
        sel = lane < (gi + 1) * POOL_GROUP
        box = jnp.where(sel, sums[POOL_WINDOWS[gi]], box)
        win = jnp.where(sel, float(POOL_WINDOWS[gi]), win)
    d = box / jnp.minimum(pos, win) - u
    yp = jnp.dot(d.astype(bf16), poolw_ref[...], preferred_element_type=f32)
    pool_out_ref[...] = (yp * pscale_ref[...]).astype(bf16)


def _mixers(cv, pin, dww, dwb, lng, lnb, pw, poolw, pscale, seq):
    t = cv.shape[0]
    tm = ROW_TILE
    tiles_per_seq = seq // tm
    const = lambda shape: pl.BlockSpec(shape, lambda i: (0,) * len(shape))
    return pl.pallas_call(
        functools.partial(_mix_kernel, tiles_per_seq=tiles_per_seq),
        grid=(t // tm,),
        in_specs=[
            pl.BlockSpec((tm, 2 * D_CONV), lambda i: (i, 0)),
            pl.BlockSpec((tm, D_POOL), lambda i: (i, 0)),
            const((HALO, D_CONV)),
            const((1, D_CONV)),
            const((1, D_CONV)),
            const((1, D_CONV)),
            const((D_CONV, D_CONV)),
            const((D_POOL, D_POOL)),
            const((1, D_POOL)),
        ],
        out_specs=[
            pl.BlockSpec((tm, D_CONV), lambda i: (i, 0)),
            pl.BlockSpec((tm, D_POOL), lambda i: (i, 0)),
        ],
        out_shape=[
            jax.ShapeDtypeStruct((t, D_CONV), bf16),
            jax.ShapeDtypeStruct((t, D_POOL), bf16),
        ],
        scratch_shapes=[pltpu.VMEM((HALO + tm, D_CONV), f32),
                        pltpu.VMEM((HALO + tm, D_POOL), f32)],
        compiler_params=pltpu.CompilerParams(
            dimension_semantics=("arbitrary",), vmem_limit_bytes=VMEM_LIMIT),
        name="mixers",
    )(cv, pin, dww, dwb, lng, lnb, pw, poolw, pscale)


def _att_kernel(qq_ref, kk_ref, vt_ref, o_ref, sa_ref, sb_ref, mta_ref, mtb_ref, m_ref, acc_ref):
    tq = qq_ref.shape[1]
    i = pl.program_id(2)
    qq = qq_ref[0]
    lane = lax.broadcasted_iota(jnp.int32, (1, 2 * LANES), 1)
    aug = lane - LANES
    row = lax.broadcasted_iota(jnp.int32, (LANES, 1), 0)

    def in_range(v, lo, hi):
        return (v >= lo) & (v < hi)

    s = _SPLITS
    keep = [
        in_range(lane, 0, HEAD_DIM) | in_range(aug, 0, s) | in_range(aug, 2 * s, 3 * s),
        in_range(lane, HEAD_DIM, LANES) | in_range(aug, s, 2 * s) | in_range(aug, 3 * s, 4 * s),
    ]
    qh = [jnp.where(kp, qq, jnp.zeros_like(qq)) for kp in keep]
    vkeep = [row < HEAD_DIM, row >= HEAD_DIM]

    def produce(s_ref, mt_ref, j):
        off = pl.multiple_of(j * tq, tq)
        kt = kk_ref[0, pl.ds(off, tq), :]
        for hh in range(2):
            st = lax.dot_general(kt, qh[hh], (((1,), (1,)), ((), ())),
                                 preferred_element_type=f32)
            s_ref[hh] = st
            mt_ref[hh] = jnp.max(st, axis=0, keepdims=True)

    def consume(s_ref, mt_ref, j, masked):
        vt = vt_ref[0, 0, j]
        for hh in range(2):
            st = s_ref[hh]
            if masked:
                kidx = lax.broadcasted_iota(jnp.int32, st.shape, 0)
                qidx = lax.broadcasted_iota(jnp.int32, st.shape, 1)
                st = jnp.where(kidx <= qidx, st, NEG)
                mt = jnp.max(st, axis=0, keepdims=True)
            else:
                mt = mt_ref[hh]
            m = m_ref[hh]
            m_new = jnp.maximum(m, mt)
            p = jnp.exp2(st - m_new).astype(bf16)
            alpha = jnp.exp2(m - m_new)
            vh = jnp.where(vkeep[hh], vt, jnp.ones_like(vt))
            acc_ref[hh] = acc_ref[hh] * alpha + jnp.dot(vh, p, preferred_element_type=f32)
            m_ref[hh] = m_new

    m_ref[...] = jnp.full(m_ref.shape, NEG, f32)
    acc_ref[...] = jnp.zeros(acc_ref.shape, f32)
    produce(sa_ref, mta_ref, 0)

    def two_steps(t, c):
        j = 2 * t
        produce(sb_ref, mtb_ref, j + 1)
        consume(sa_ref, mta_ref, j, False)
        produce(sa_ref, mta_ref, j + 2)
        consume(sb_ref, mtb_ref, j + 1, False)
        return c

    lax.fori_loop(0, i // 2, two_steps, 0)
    odd = (i % 2) == 1

    @pl.when(odd)
    def _():
        produce(sb_ref, mtb_ref, i)
        consume(sa_ref, mta_ref, i - 1, False)
        consume(sb_ref, mtb_ref, i, True)

    @pl.when(jnp.logical_not(odd))
    def _():
        consume(sa_ref, mta_ref, i, True)

    acc0 = acc_ref[0]
    acc1 = acc_ref[1]
    out0 = acc0 / acc0[HEAD_DIM:HEAD_DIM + 1, :]
    out1 = acc1 / acc1[0:1, :]
    ot = jnp.where(row < HEAD_DIM, out0, out1)
    o_ref[0] = ot.T.astype(bf16)


def _attention(qq, kk, vt, batch, seq):
    tq = ROW_TILE
    nq = seq // tq
    return pl.pallas_call(
        _att_kernel,
        grid=(batch, N_PAIRS, nq),
        in_specs=[
            pl.BlockSpec((1, tq, 2 * LANES), lambda b, p, i: (b, i, p)),
            pl.BlockSpec((1, seq, 2 * LANES), lambda b, p, i: (b, 0, p)),
            pl.BlockSpec((1, 1, nq, LANES, tq), lambda b, p, i: (b, p, 0, 0, 0)),
        ],
        out_specs=pl.BlockSpec((1, tq, LANES), lambda b, p, i: (b, i, p)),
        out_shape=jax.ShapeDtypeStruct((batch, seq, D_ATT), bf16),
        scratch_shapes=[pltpu.VMEM((2, tq, tq), f32), pltpu.VMEM((2, tq, tq), f32),
                        pltpu.VMEM((2, 1, tq), f32), pltpu.VMEM((2, 1, tq), f32),
                        pltpu.VMEM((2, 1, tq), f32), pltpu.VMEM((2, LANES, tq), f32)],
        compiler_params=pltpu.CompilerParams(
            dimension_semantics=("arbitrary", "arbitrary", "arbitrary"),
            vmem_limit_bytes=VMEM_LIMIT),
        name="fox_attention",
    )(qq, kk, vt)


def _ffn_kernel(x_ref, att_ref, conv_ref, pool_ref, wo_ref, g2_ref, wup_ref, dw_ref, wdn_ref,
                out_ref, tail_ref, act_ref, *, tiles_per_seq):
    tm = x_ref.shape[0]

    @pl.when(pl.program_id(0) % tiles_per_seq == 0)
    def _():
        tail_ref[...] = jnp.zeros_like(tail_ref)

    mix = jnp.concatenate([att_ref[...], conv_ref[...], pool_ref[...]], axis=1)
    x1 = x_ref[...] + jnp.dot(mix, wo_ref[...], preferred_element_type=f32)
    ms = jnp.mean(x1 * x1, axis=-1, keepdims=True)
    h2 = (x1 * lax.rsqrt(ms + EPS) * g2_ref[...]).astype(bf16)

    for c in range(N_FF_CHUNKS):
        up = jnp.dot(h2, wup_ref[c], preferred_element_type=f32)
        ext = jnp.concatenate([tail_ref[c], up], axis=0)
        tail_ref[c] = up[tm - SUBLANES:tm, :]
        s1 = pltpu.roll(ext, 1, 0)[SUBLANES:, :]
        s2 = pltpu.roll(ext, 2, 0)[SUBLANES:, :]
        w = dw_ref[c]
        cvd = w[0:1, :] * s2 + w[1:2, :] * s1 + w[2:3, :] * up
        gate = cvd[:, :FF_CHUNK]
        val = cvd[:, FF_CHUNK:]
        act_ref[:, c * FF_CHUNK:(c + 1) * FF_CHUNK] = (
            gate * jax.nn.sigmoid(gate) * val).astype(bf16)

    out_ref[...] = x1 + jnp.dot(act_ref[...], wdn_ref[...], preferred_element_type=f32)


def _ffn(x2, att, conv, pool, wo, g2, wup, dw, wdn, seq):
    t = x2.shape[0]
    tm = ROW_TILE
    tiles_per_seq = seq // tm
    resident = lambda shape: pl.BlockSpec(shape, lambda i: (0,) * len(shape),
                                          pipeline_mode=pl.Buffered(1))
    return pl.pallas_call(
        functools.partial(_ffn_kernel, tiles_per_seq=tiles_per_seq),
        grid=(t // tm,),
        in_specs=[
            pl.BlockSpec((tm, D_MODEL), lambda i: (i, 0)),
            pl.BlockSpec((tm, D_ATT), lambda i: (i, 0)),
            pl.BlockSpec((tm, D_CONV), lambda i: (i, 0)),
            pl.BlockSpec((tm, D_POOL), lambda i: (i, 0)),
            resident((D_MODEL, D_MODEL)),
            resident((1, D_MODEL)),
            resident((N_FF_CHUNKS, D_MODEL, 2 * FF_CHUNK)),
            resident((N_FF_CHUNKS, SUBLANES, 2 * FF_CHUNK)),
            resident((D_FF, D_MODEL)),
        ],
        out_specs=pl.BlockSpec((tm, D_MODEL), lambda i: (i, 0)),
        out_shape=jax.ShapeDtypeStruct((t, D_MODEL), f32),
        scratch_shapes=[pltpu.VMEM((N_FF_CHUNKS, SUBLANES, 2 * FF_CHUNK), f32),
                        pltpu.VMEM((tm, D_FF), bf16)],
        compiler_params=pltpu.CompilerParams(
            dimension_semantics=("arbitrary",), vmem_limit_bytes=VMEM_LIMIT),
        name="out_proj_ffn",
    )(x2, att, conv, pool, wo, g2, wup, dw, wdn)


def _constants():
    grp = np.kron(np.eye(256 // HEAD_DIM, dtype=np.float32),
                  np.full((HEAD_DIM, HEAD_DIM), 1.0 / HEAD_DIM, np.float32))
    tri = np.tril(np.ones((ROW_TILE, ROW_TILE), np.float32))
    eq, ek = _placement_matrices()
    return tuple(jnp.asarray(a, bf16) for a in (grp, tri, eq, ek))


def _chunk_ff(w):
    lead = w.shape[:-1]
    gate = w[..., :D_FF].reshape(lead + (N_FF_CHUNKS, FF_CHUNK))
    val = w[..., D_FF:].reshape(lead + (N_FF_CHUNKS, FF_CHUNK))
    both = jnp.concatenate([gate, val], axis=-1)
    return jnp.moveaxis(both, -2, 0)


def kernel(x, norm1_g, w_in, b_f, q_norm_g, k_norm_g, conv_dw_w, conv_dw_b, conv_ln_g, conv_ln_b,
           conv_pw_w, pool_w, pool_scale, w_out, norm2_g, w_up, ffn_dw_w, w_down):
    batch, seq, _ = x.shape
    depth = w_in.shape[0]
    assert seq % ROW_TILE == 0
    consts = _constants()
    x2 = x.reshape(batch * seq, D_MODEL)

    for l in range(depth):
        wi = w_in[l]
        o = 3 * D_ATT
        w_fg = wi[:, o:o + ATT_HEADS]
        o += ATT_HEADS
        w_cv = wi[:, o:o + 2 * D_CONV]
        o += 2 * D_CONV
        w_pl = wi[:, o:o + D_POOL]
        fg_pad = jnp.zeros((D_MODEL, FG_LANES - _SPLITS * ATT_HEADS), f32)
        w_cat = jnp.concatenate([wi[:, :3 * D_ATT], w_cv, w_pl] + [w_fg] * _SPLITS + [fg_pad],
                                axis=1).astype(bf16)
        bf_pad = jnp.concatenate([b_f[l]] * _SPLITS
                                 + [jnp.zeros((FG_LANES - _SPLITS * ATT_HEADS,), f32)])[None, :]
        gq = jnp.tile(q_norm_g[l], ATT_HEADS)[None, :]
        gk = jnp.tile(k_norm_g[l], ATT_HEADS)[None, :]

        qq, kk, vt, cv, pin = _in_proj(x2, norm1_g[l][None, :], w_cat, bf_pad, gq, gk, consts,
                                       batch, seq)

        dww = jnp.concatenate([conv_dw_w[l], jnp.zeros((HALO - CONV_WIDTH, D_CONV), f32)], axis=0)
        poolw = jax.scipy.linalg.block_diag(*[pool_w[l, g] for g in range(len(POOL_WINDOWS))])
        conv, pool = _mixers(cv, pin, dww, conv_dw_b[l][None, :], conv_ln_g[l][None, :],
                             conv_ln_b[l][None, :], conv_pw_w[l].astype(bf16),
                             poolw.astype(bf16), pool_scale[l][None, :], seq)

        att = _attention(qq.reshape(batch, seq, 2 * D_ATT), kk.reshape(batch, seq, 2 * D_ATT),
                         vt, batch, seq).reshape(batch * seq, D_ATT)

        wup = _chunk_ff(w_up[l]).astype(bf16)
        dw = _chunk_ff(ffn_dw_w[l])
        dw = jnp.concatenate(
            [dw, jnp.zeros((N_FF_CHUNKS, SUBLANES - FFN_CONV_WIDTH, 2 * FF_CHUNK), f32)], axis=1)
        x2 = _ffn(x2, att, conv, pool, w_out[l].astype(bf16), norm2_g[l][None, :], wup, dw,
                  w_down[l].astype(bf16), seq)

    return x2.reshape(batch, seq, D_MODEL)
```

```python
import functools

import numpy as np
import jax
import jax.numpy as jnp
from jax import lax
from jax.experimental import pallas as pl
from jax.experimental.pallas import tpu as pltpu

D_MODEL = 1024
ATT_HEADS = 8
HEAD_DIM = 64
D_ATT = ATT_HEADS * HEAD_DIM
N_PAIRS = ATT_HEADS // 2
ATT_SCALE = HEAD_DIM ** -0.5
D_CONV = 256
CONV_WIDTH = 31
POOL_WINDOWS = (2, 4, 8, 16)
POOL_GROUP = 64
D_POOL = len(POOL_WINDOWS) * POOL_GROUP
D_FF = 2816
FFN_CONV_WIDTH = 3
EPS = 1e-6
NEG = -1e30
LOG2E = 1.4426950408889634

LANES = 128
SUBLANES = 8
FG_LANES = LANES
N_MIX_IN = 2 * D_CONV + D_POOL + FG_LANES
N_PROJ = N_MIX_IN + 3 * D_ATT
ROW_TILE = 512
HALO = 32
POOL_HALO = 16
FF_CHUNK = 256
N_FF_CHUNKS = D_FF // FF_CHUNK
ATT_UNROLL = 4
VMEM_LIMIT = 56 * 1024 * 1024

f32 = jnp.float32
bf16 = jnp.bfloat16

_SPLITS = 3


def _placement_matrices():
    eq = np.zeros((LANES, N_PAIRS * LANES), np.float32)
    ek = np.zeros((LANES, N_PAIRS * LANES), np.float32)
    for h in range(ATT_HEADS):
        p, hh = divmod(h, 2)
        for s in range(_SPLITS):
            src = s * ATT_HEADS + h
            ek[src, p * LANES + hh * _SPLITS + s] = -1.0
            eq[src, p * LANES + 2 * _SPLITS + hh * _SPLITS + s] = 1.0
    one_lane = _SPLITS * ATT_HEADS
    for p in range(N_PAIRS):
        ek[one_lane, p * LANES + 2 * _SPLITS:p * LANES + 4 * _SPLITS] = 1.0
        eq[one_lane, p * LANES:p * LANES + 2 * _SPLITS] = 1.0
    return eq, ek


def _split3(c):
    hi = c.astype(bf16).astype(f32)
    r1 = c - hi
    mid = r1.astype(bf16).astype(f32)
    lo = r1 - mid
    return hi, mid, lo


def _pack_groups(lane, a0, a1, a2, rest):
    return jnp.where(lane < ATT_HEADS, a0,
                     jnp.where(lane < 2 * ATT_HEADS, a1,
                               jnp.where(lane < 3 * ATT_HEADS, a2, rest)))


def _conformer_conv(cv, hbuf, dww_ref, dwb_ref, lng_ref, lnb_ref, pw_ref):
    tm = cv.shape[0]
    hbuf[HALO:HALO + tm, :] = cv[:, :D_CONV] * jax.nn.sigmoid(cv[:, D_CONV:])
    ext = tm + SUBLANES
    acc = None
    for r in range(SUBLANES):
        part = None
        for a in range((CONV_WIDTH - 1 - r) // SUBLANES + 1):
            tap = CONV_WIDTH - 1 - (SUBLANES * a + r)
            lo = HALO - SUBLANES - SUBLANES * a
            term = dww_ref[tap:tap + 1, :] * hbuf[lo:lo + ext, :]
            part = term if part is None else part + term
        if r:
            part = pltpu.roll(part, r, 0)
        part = part[SUBLANES:, :]
        acc = part if acc is None else acc + part
    acc = acc + dwb_ref[...]
    hbuf[0:HALO, :] = hbuf[tm:tm + HALO, :]
    mu = jnp.mean(acc, axis=-1, keepdims=True)
    cen = acc - mu
    var = jnp.mean(cen * cen, axis=-1, keepdims=True)
    y = cen * lax.rsqrt(var + EPS) * lng_ref[...] + lnb_ref[...]
    y = y * jax.nn.sigmoid(y)
    return jnp.dot(y.astype(bf16), pw_ref[...], preferred_element_type=f32)


def _multiscale_pool(u, ubuf, poolw_ref, pscale_ref, t_in_seq):
    tm = u.shape[0]
    ubuf[HALO:HALO + tm, :] = u
    e = ubuf[HALO - POOL_HALO:HALO + tm, :]
    sums = {}
    w = 1
    while w < max(POOL_WINDOWS):
        e = e + pltpu.roll(e, w, 0)
        w *= 2
        sums[w] = e[POOL_HALO:, :]
    ubuf[0:HALO, :] = ubuf[tm:tm + HALO, :]
    lane = lax.broadcasted_iota(jnp.int32, (tm, D_POOL), 1)
    row = lax.broadcasted_iota(jnp.int32, (tm, D_POOL), 0)
    pos = (row + t_in_seq * tm + 1).astype(f32)
    box = sums[POOL_WINDOWS[-1]]
    win = jnp.full((tm, D_POOL), float(POOL_WINDOWS[-1]), f32)
    for gi in range(len(POOL_WINDOWS) - 2, -1, -1):
        sel = lane < (gi + 1) * POOL_GROUP
        box = jnp.where(sel, sums[POOL_WINDOWS[gi]], box)
        win = jnp.where(sel, float(POOL_WINDOWS[gi]), win)
    d = box / jnp.minimum(pos, win) - u
    yp = jnp.dot(d.astype(bf16), poolw_ref[...], preferred_element_type=f32)
    return yp * pscale_ref[...]


def _in_kernel(x_ref, g1_ref, w_ref, bf_ref, gq_ref, gk_ref, grp_ref, tri_ref, eq_ref, ek_ref,
               dww_ref, dwb_ref, lng_ref, lnb_ref, pw_ref, poolw_ref, pscale_ref,
               qq_ref, kk_ref, vt_ref, conv_ref, pool_ref, carry_ref, hbuf, ubuf, *, tiles_per_seq):
    tm = x_ref.shape[0]
    t_in_seq = pl.program_id(0) % tiles_per_seq

    @pl.when(t_in_seq == 0)
    def _():
        carry_ref[...] = jnp.zeros_like(carry_ref)
        hbuf[0:HALO, :] = jnp.zeros((HALO, D_CONV), f32)
        ubuf[0:HALO, :] = jnp.zeros((HALO, D_POOL), f32)

    x = x_ref[...]
    ms = jnp.mean(x * x, axis=-1, keepdims=True)
    h = (x * lax.rsqrt(ms + EPS) * g1_ref[...]).astype(bf16)
    proj_m = jnp.dot(h, w_ref[:, :N_MIX_IN], preferred_element_type=f32)
    proj_a = jnp.dot(h, w_ref[:, N_MIX_IN:], preferred_element_type=f32)
    cv = proj_m[:, 0:2 * D_CONV]
    pin = proj_m[:, 2 * D_CONV:2 * D_CONV + D_POOL]
    fg = proj_m[:, 2 * D_CONV + D_POOL:N_MIX_IN]
    q = proj_a[:, 0:D_ATT]
    k = proj_a[:, D_ATT:2 * D_ATT]
    v = proj_a[:, 2 * D_ATT:3 * D_ATT]

    grp = grp_ref[...]

    def head_norm(t, g_ref, mult):
        tt = t * t
        hi = tt.astype(bf16)
        lo = (tt - hi.astype(f32)).astype(bf16)
        halves = []
        for c in range(D_ATT // 256):
            sl = slice(c * 256, (c + 1) * 256)
            halves.append(jnp.dot(hi[:, sl], grp, preferred_element_type=f32)
                          + jnp.dot(lo[:, sl], grp, preferred_element_type=f32))
        msq = jnp.concatenate(halves, axis=1)
        y = t * lax.rsqrt(msq + EPS) * g_ref[...]
        if mult != 1.0:
            y = y * mult
        return y.astype(bf16)

    qn = head_norm(q, gq_ref, ATT_SCALE * LOG2E)
    kn = head_norm(k, gk_ref, 1.0)

    lane = lax.broadcasted_iota(jnp.int32, (tm, FG_LANES), 1)
    lf = jax.nn.log_sigmoid(fg + bf_ref[...]) * LOG2E
    l0, l1, l2 = _split3(lf)
    packed = _pack_groups(lane, l0, l1, l2, 0.0).astype(bf16)
    part = jnp.dot(tri_ref[...], packed, preferred_element_type=f32)
    g = ATT_HEADS
    tot = (part + pltpu.roll(part, g, 1) + pltpu.roll(part, 2 * g, 1)
           + pltpu.roll(part, FG_LANES - g, 1) + pltpu.roll(part, FG_LANES - 2 * g, 1))
    cum = tot + carry_ref[0:1, :]
    carry_ref[...] = jnp.broadcast_to(cum[tm - 1:tm, :], carry_ref.shape)

    c0, c1, c2 = _split3(cum)
    ones_lane = jnp.where(lane == _SPLITS * ATT_HEADS, 1.0, 0.0)
    aug_src = _pack_groups(lane, c0, c1, c2, ones_lane).astype(bf16)
    qa = jnp.dot(aug_src, eq_ref[...], preferred_element_type=f32).astype(bf16)
    ka = jnp.dot(aug_src, ek_ref[...], preferred_element_type=f32).astype(bf16)

    def interleave(a, b):
        pieces = []
        for p in range(N_PAIRS):
            sl = slice(p * LANES, (p + 1) * LANES)
            pieces += [a[:, sl], b[:, sl]]
        return jnp.concatenate(pieces, axis=1)

    qq_ref[...] = interleave(qn, qa)
    kk_ref[...] = interleave(kn, ka)

    vt = v.T.astype(bf16)
    for p in range(N_PAIRS):
        vt_ref[0, p, 0] = vt[p * LANES:(p + 1) * LANES, :]

    conv_ref[...] = _conformer_conv(cv, hbuf, dww_ref, dwb_ref, lng_ref, lnb_ref,
                                    pw_ref).astype(bf16)
    pool_ref[...] = _multiscale_pool(pin, ubuf, poolw_ref, pscale_ref, t_in_seq).astype(bf16)


def _in_proj_mix(x2, g1, w_cat, bf_pad, gq, gk, consts, mix_params, batch, seq):
    t = x2.shape[0]
    tm = ROW_TILE
    tiles_per_seq = seq // tm
    n_tiles = t // tm
    const = lambda shape: pl.BlockSpec(shape, lambda i: (0,) * len(shape))
    return pl.pallas_call(
        functools.partial(_in_kernel, tiles_per_seq=tiles_per_seq),
        grid=(n_tiles,),
        in_specs=[
            pl.BlockSpec((tm, D_MODEL), lambda i: (i, 0)),
            const((1, D_MODEL)),
            const((D_MODEL, N_PROJ)),
            const((1, FG_LANES)),
            const((1, D_ATT)),
            const((1, D_ATT)),
            const((256, 256)),
            const((tm, tm)),
            const((LANES, N_PAIRS * LANES)),
            const((LANES, N_PAIRS * LANES)),
            const((HALO, D_CONV)),
            const((1, D_CONV)),
            const((1, D_CONV)),
            const((1, D_CONV)),
            const((D_CONV, D_CONV)),
            const((D_POOL, D_POOL)),
            const((1, D_POOL)),
        ],
        out_specs=[
            pl.BlockSpec((tm, 2 * D_ATT), lambda i: (i, 0)),
            pl.BlockSpec((tm, 2 * D_ATT), lambda i: (i, 0)),
            pl.BlockSpec((1, N_PAIRS, 1, LANES, tm),
                         lambda i: (i // tiles_per_seq, 0, i % tiles_per_seq, 0, 0)),
            pl.BlockSpec((tm, D_CONV), lambda i: (i, 0)),
            pl.BlockSpec((tm, D_POOL), lambda i: (i, 0)),
        ],
        out_shape=[
            jax.ShapeDtypeStruct((t, 2 * D_ATT), bf16),
            jax.ShapeDtypeStruct((t, 2 * D_ATT), bf16),
            jax.ShapeDtypeStruct((batch, N_PAIRS, tiles_per_seq, LANES, tm), bf16),
            jax.ShapeDtypeStruct((t, D_CONV), bf16),
            jax.ShapeDtypeStruct((t, D_POOL), bf16),
        ],
        scratch_shapes=[pltpu.VMEM((SUBLANES, FG_LANES), f32),
                        pltpu.VMEM((HALO + tm, D_CONV), f32),
                        pltpu.VMEM((HALO + tm, D_POOL), f32)],
        compiler_params=pltpu.CompilerParams(
            dimension_semantics=("arbitrary",), vmem_limit_bytes=VMEM_LIMIT),
        name="in_proj_mix",
    )(x2, g1, w_cat, bf_pad, gq, gk, *consts, *mix_params)


def _block_schedule(nq):
    blocks = [(i, j) for i in range(nq) for j in range(i)]
    n_unmasked = len(blocks)
    blocks += [(i, i) for i in range(nq)]
    blocks.append((0, 0))
    bi = np.asarray([b[0] for b in blocks], np.int32)
    bj = np.asarray([b[1] for b in blocks], np.int32)
    return bi, bj, n_unmasked


def _att_kernel(bi_ref, bj_ref, qq_ref, kk_ref, vt_ref, o_ref, qh_ref, s0_ref, s1_ref,
                mt0_ref, mt1_ref, m_ref, acc_ref, *, n_unmasked):
    nq, _, tq, _ = qh_ref.shape
    n_blocks = n_unmasked + nq
    lane = lax.broadcasted_iota(jnp.int32, (1, 2 * LANES), 1)
    aug = lane - LANES
    row = lax.broadcasted_iota(jnp.int32, (LANES, 1), 0)

    def in_range(v, lo, hi):
        return (v >= lo) & (v < hi)

    s = _SPLITS
    keep = [
        in_range(lane, 0, HEAD_DIM) | in_range(aug, 0, s) | in_range(aug, 2 * s, 3 * s),
        in_range(lane, HEAD_DIM, LANES) | in_range(aug, s, 2 * s) | in_range(aug, 3 * s, 4 * s),
    ]
    for it in range(nq):
        qq = qq_ref[0, it * tq:(it + 1) * tq, :]
        for hh in range(2):
            qh_ref[it, hh] = jnp.where(keep[hh], qq, jnp.zeros_like(qq))
    vkeep = [row < HEAD_DIM, row >= HEAD_DIM]
    m_ref[...] = jnp.full(m_ref.shape, NEG, f32)
    acc_ref[...] = jnp.zeros(acc_ref.shape, f32)
    s_refs = (s0_ref, s1_ref)
    mt_refs = (mt0_ref, mt1_ref)

    def produce(slot, n):
        i = bi_ref[n]
        off = pl.multiple_of(bj_ref[n] * tq, tq)
        kt = kk_ref[0, pl.ds(off, tq), :]
        for hh in range(2):
            st = lax.dot_general(kt, qh_ref[i, hh], (((1,), (1,)), ((), ())),
                                 preferred_element_type=f32)
            s_refs[slot][hh] = st
            mt_refs[slot][hh] = jnp.max(st, axis=0, keepdims=True)

    def consume(slot, n, masked):
        i = bi_ref[n]
        vt = vt_ref[0, 0, bj_ref[n]]
        for hh in range(2):
            st = s_refs[slot][hh]
            if masked:
                kidx = lax.broadcasted_iota(jnp.int32, st.shape, 0)
                qidx = lax.broadcasted_iota(jnp.int32, st.shape, 1)
                st = jnp.where(kidx <= qidx, st, NEG)
                mt = jnp.max(st, axis=0, keepdims=True)
            else:
                mt = mt_refs[slot][hh]
            m = m_ref[i, hh]
            m_new = jnp.maximum(m, mt)
            p = jnp.exp2(st - m_new).astype(bf16)
            alpha = jnp.exp2(m - m_new)
            vh = jnp.where(vkeep[hh], vt, jnp.ones_like(vt))
            acc_ref[i, hh] = acc_ref[i, hh] * alpha + jnp.dot(vh, p, preferred_element_type=f32)
            m_ref[i, hh] = m_new

    def run(start, count, masked):
        par = start % 2
        full = count // ATT_UNROLL
        if full:
            def body(t, c):
                base = start + t * ATT_UNROLL
                for u in range(ATT_UNROLL):
                    produce((u + 1 + par) % 2, base + u + 1)
                    consume((u + par) % 2, base + u, masked)
                return c
            lax.fori_loop(0, full, body, 0)
        for n in range(start + full * ATT_UNROLL, start + count):
            produce((n + 1) % 2, n + 1)
            consume(n % 2, n, masked)

    produce(0, 0)
    run(0, n_unmasked, False)
    run(n_unmasked, n_blocks - n_unmasked, True)

    for it in range(nq):
        acc0 = acc_ref[it, 0]
        acc1 = acc_ref[it, 1]
        out0 = acc0 / acc0[HEAD_DIM:HEAD_DIM + 1, :]
        out1 = acc1 / acc1[0:1, :]
        ot = jnp.where(row < HEAD_DIM, out0, out1)
        o_ref[0, it * tq:(it + 1) * tq, :] = ot.T.astype(bf16)


def _attention(qq, kk, vt, batch, seq):
    tq = ROW_TILE
    nq = seq // tq
    bi, bj, n_unmasked = _block_schedule(nq)
    grid_spec = pltpu.PrefetchScalarGridSpec(
        num_scalar_prefetch=2,
        grid=(batch, N_PAIRS),
        in_specs=[
            pl.BlockSpec((1, seq, 2 * LANES), lambda b, p, bi, bj: (b, 0, p)),
            pl.BlockSpec((1, seq, 2 * LANES), lambda b, p, bi, bj: (b, 0, p)),
            pl.BlockSpec((1, 1, nq, LANES, tq), lambda b, p, bi, bj: (b, p, 0, 0, 0)),
        ],
        out_specs=pl.BlockSpec((1, seq, LANES), lambda b, p, bi, bj: (b, 0, p)),
        scratch_shapes=[pltpu.VMEM((nq, 2, tq, 2 * LANES), bf16),
                        pltpu.VMEM((2, tq, tq), f32), pltpu.VMEM((2, tq, tq), f32),
                        pltpu.VMEM((2, 1, tq), f32), pltpu.VMEM((2, 1, tq), f32),
                        pltpu.VMEM((nq, 2, 1, tq), f32), pltpu.VMEM((nq, 2, LANES, tq), f32)],
    )
    return pl.pallas_call(
        functools.partial(_att_kernel, n_unmasked=n_unmasked),
        grid_spec=grid_spec,
        out_shape=jax.ShapeDtypeStruct((batch, seq, D_ATT), bf16),
        compiler_params=pltpu.CompilerParams(
            dimension_semantics=("arbitrary", "arbitrary"), vmem_limit_bytes=VMEM_LIMIT),
        name="fox_attention",
    )(jnp.asarray(bi), jnp.asarray(bj), qq, kk, vt)


def _ffn_kernel(x_ref, att_ref, conv_ref, pool_ref, wo_ref, g2_ref, wup_ref, dw_ref, wdn_ref,
                out_ref, tail_ref, act_ref, *, tiles_per_seq):
    tm = x_ref.shape[0]

    @pl.when(pl.program_id(0) % tiles_per_seq == 0)
    def _():
        tail_ref[...] = jnp.zeros_like(tail_ref)

    mix = jnp.concatenate([att_ref[...], conv_ref[...], pool_ref[...]], axis=1)
    x1 = x_ref[...] + jnp.dot(mix, wo_ref[...], preferred_element_type=f32)
    ms = jnp.mean(x1 * x1, axis=-1, keepdims=True)
    h2 = (x1 * lax.rsqrt(ms + EPS) * g2_ref[...]).astype(bf16)

    for c in range(N_FF_CHUNKS):
        up = jnp.dot(h2, wup_ref[c], preferred_element_type=f32)
        ext = jnp.concatenate([tail_ref[c], up], axis=0)
        tail_ref[c] = up[tm - SUBLANES:tm, :]
        s1 = pltpu.roll(ext, 1, 0)[SUBLANES:, :]
        s2 = pltpu.roll(ext, 2, 0)[SUBLANES:, :]
        w = dw_ref[c]
        cvd = w[0:1, :] * s2 + w[1:2, :] * s1 + w[2:3, :] * up
        gate = cvd[:, :FF_CHUNK]
        val = cvd[:, FF_CHUNK:]
        act_ref[:, c * FF_CHUNK:(c + 1) * FF_CHUNK] = (
            gate * jax.nn.sigmoid(gate) * val).astype(bf16)

    out_ref[...] = x1 + jnp.dot(act_ref[...], wdn_ref[...], preferred_element_type=f32)


def _ffn(x2, att, conv, pool, wo, g2, wup, dw, wdn, seq):
    t = x2.shape[0]
    tm = ROW_TILE
    tiles_per_seq = seq // tm
    resident = lambda shape: pl.BlockSpec(shape, lambda i: (0,) * len(shape),
                                          pipeline_mode=pl.Buffered(1))
    return pl.pallas_call(
        functools.partial(_ffn_kernel, tiles_per_seq=tiles_per_seq),
        grid=(t // tm,),
        in_specs=[
            pl.BlockSpec((tm, D_MODEL), lambda i: (i, 0)),
            pl.BlockSpec((tm, D_ATT), lambda i: (i, 0)),
            pl.BlockSpec((tm, D_CONV), lambda i: (i, 0)),
            pl.BlockSpec((tm, D_POOL), lambda i: (i, 0)),
            resident((D_MODEL, D_MODEL)),
            resident((1, D_MODEL)),
            resident((N_FF_CHUNKS, D_MODEL, 2 * FF_CHUNK)),
            resident((N_FF_CHUNKS, SUBLANES, 2 * FF_CHUNK)),
            resident((D_FF, D_MODEL)),
        ],
        out_specs=pl.BlockSpec((tm, D_MODEL), lambda i: (i, 0)),
        out_shape=jax.ShapeDtypeStruct((t, D_MODEL), f32),
        scratch_shapes=[pltpu.VMEM((N_FF_CHUNKS, SUBLANES, 2 * FF_CHUNK), f32),
                        pltpu.VMEM((tm, D_FF), bf16)],
        compiler_params=pltpu.CompilerParams(
            dimension_semantics=("arbitrary",), vmem_limit_bytes=VMEM_LIMIT),
        name="out_proj_ffn",
    )(x2, att, conv, pool, wo, g2, wup, dw, wdn)


def _constants():
    grp = np.kron(np.eye(256 // HEAD_DIM, dtype=np.float32),
                  np.full((HEAD_DIM, HEAD_DIM), 1.0 / HEAD_DIM, np.float32))
    tri = np.tril(np.ones((ROW_TILE, ROW_TILE), np.float32))
    eq, ek = _placement_matrices()
    return tuple(jnp.asarray(a, bf16) for a in (grp, tri, eq, ek))


def _chunk_ff(w):
    lead = w.shape[:-1]
    gate = w[..., :D_FF].reshape(lead + (N_FF_CHUNKS, FF_CHUNK))
    val = w[..., D_FF:].reshape(lead + (N_FF_CHUNKS, FF_CHUNK))
    both = jnp.concatenate([gate, val], axis=-1)
    return jnp.moveaxis(both, -2, 0)


def kernel(x, norm1_g, w_in, b_f, q_norm_g, k_norm_g, conv_dw_w, conv_dw_b, conv_ln_g, conv_ln_b,
           conv_pw_w, pool_w, pool_scale, w_out, norm2_g, w_up, ffn_dw_w, w_down):
    batch, seq, _ = x.shape
    depth = w_in.shape[0]
    assert seq % ROW_TILE == 0
    consts = _constants()
    x2 = x.reshape(batch * seq, D_MODEL)

    for l in range(depth):
        wi = w_in[l]
        o = 3 * D_ATT
        w_fg = wi[:, o:o + ATT_HEADS]
        o += ATT_HEADS
        w_cv = wi[:, o:o + 2 * D_CONV]
        o += 2 * D_CONV
        w_pl = wi[:, o:o + D_POOL]
        fg_pad = jnp.zeros((D_MODEL, FG_LANES - _SPLITS * ATT_HEADS), f32)
        w_cat = jnp.concatenate([w_cv, w_pl] + [w_fg] * _SPLITS + [fg_pad, wi[:, :3 * D_ATT]],
                                axis=1).astype(bf16)
        bf_pad = jnp.concatenate([b_f[l]] * _SPLITS
                                 + [jnp.zeros((FG_LANES - _SPLITS * ATT_HEADS,), f32)])[None, :]
        gq = jnp.tile(q_norm_g[l], ATT_HEADS)[None, :]
        gk = jnp.tile(k_norm_g[l], ATT_HEADS)[None, :]
        dww = jnp.concatenate([conv_dw_w[l], jnp.zeros((HALO - CONV_WIDTH, D_CONV), f32)], axis=0)
        poolw = jax.scipy.linalg.block_diag(*[pool_w[l, g] for g in range(len(POOL_WINDOWS))])
        mix_params = (dww, conv_dw_b[l][None, :], conv_ln_g[l][None, :], conv_ln_b[l][None, :],
                      conv_pw_w[l].astype(bf16), poolw.astype(bf16), pool_scale[l][None, :])

        qq, kk, vt, conv, pool = _in_proj_mix(x2, norm1_g[l][None, :], w_cat, bf_pad, gq, gk,
                                              consts, mix_params, batch, seq)

        att = _attention(qq.reshape(batch, seq, 2 * D_ATT), kk.reshape(batch, seq, 2 * D_ATT),
                         vt, batch, seq).reshape(batch * seq, D_ATT)

        wup = _chunk_ff(w_up[l]).astype(bf16)
        dw = _chunk_ff(ffn_dw_w[l])
        dw = jnp.concatenate(
            [dw, jnp.zeros((N_FF_CHUNKS, SUBLANES - FFN_CONV_WIDTH, 2 * FF_CHUNK), f32)], axis=1)
        x2 = _ffn(x2, att, conv, pool, w_out[l].astype(bf16), norm2_g[l][None, :], wup, dw,
                  w_down[l].astype(bf16), seq)

    return x2.reshape(batch, seq, D_MODEL)
```

```python
import functools

import numpy as np
import jax
import jax.numpy as jnp
from jax import lax
from jax.experimental import pallas as pl
from jax.experimental.pallas import tpu as pltpu

D_MODEL = 1024
ATT_HEADS = 8
HEAD_DIM = 64
D_ATT = ATT_HEADS * HEAD_DIM
N_PAIRS = ATT_HEADS // 2
ATT_SCALE = HEAD_DIM ** -0.5
D_CONV = 256
CONV_WIDTH = 31
POOL_WINDOWS = (2, 4, 8, 16)
POOL_GROUP = 64
D_POOL = len(POOL_WINDOWS) * POOL_GROUP
D_FF = 2816
FFN_CONV_WIDTH = 3
EPS = 1e-6
NEG = -1e30
LOG2E = 1.4426950408889634

LANES = 128
SUBLANES = 8
FG_LANES = LANES
N_MIX_IN = 2 * D_CONV + D_POOL + FG_LANES
N_PROJ = N_MIX_IN + 3 * D_ATT
ROW_TILE = 512
HALO = 32
POOL_HALO = 16
FF_CHUNK = 256
N_FF_CHUNKS = D_FF // FF_CHUNK
ATT_UNROLL = 4
VMEM_LIMIT = 56 * 1024 * 1024

f32 = jnp.float32
bf16 = jnp.bfloat16

_SPLITS = 3


def _placement_matrices():
    eq = np.zeros((LANES, N_PAIRS * LANES), np.float32)
    ek = np.zeros((LANES, N_PAIRS * LANES), np.float32)
    for h in range(ATT_HEADS):
        p, hh = divmod(h, 2)
        for s in range(_SPLITS):
            src = s * ATT_HEADS + h
            ek[src, p * LANES + hh * _SPLITS + s] = -1.0
            eq[src, p * LANES + 2 * _SPLITS + hh * _SPLITS + s] = 1.0
    one_lane = _SPLITS * ATT_HEADS
    for p in range(N_PAIRS):
        ek[one_lane, p * LANES + 2 * _SPLITS:p * LANES + 4 * _SPLITS] = 1.0
        eq[one_lane, p * LANES:p * LANES + 2 * _SPLITS] = 1.0
    return eq, ek


def _split3(c):
    hi = c.astype(bf16).astype(f32)
    r1 = c - hi
    mid = r1.astype(bf16).astype(f32)
    lo = r1 - mid
    return hi, mid, lo


def _pack_groups(lane, a0, a1, a2, rest):
    return jnp.where(lane < ATT_HEADS, a0,
                     jnp.where(lane < 2 * ATT_HEADS, a1,
                               jnp.where(lane < 3 * ATT_HEADS, a2, rest)))


def _conformer_conv(cv, hbuf, dww_ref, dwb_ref, lng_ref, lnb_ref, pw_ref):
    tm = cv.shape[0]
    hbuf[HALO:HALO + tm, :] = cv[:, :D_CONV] * jax.nn.sigmoid(cv[:, D_CONV:])
    ext = tm + SUBLANES
    acc = None
    for r in range(SUBLANES):
        part = None
        for a in range((CONV_WIDTH - 1 - r) // SUBLANES + 1):
            tap = CONV_WIDTH - 1 - (SUBLANES * a + r)
            lo = HALO - SUBLANES - SUBLANES * a
            term = dww_ref[tap:tap + 1, :] * hbuf[lo:lo + ext, :]
            part = term if part is None else part + term
        if r:
            part = pltpu.roll(part, r, 0)
        part = part[SUBLANES:, :]
        acc = part if acc is None else acc + part
    acc = acc + dwb_ref[...]
    hbuf[0:HALO, :] = hbuf[tm:tm + HALO, :]
    mu = jnp.mean(acc, axis=-1, keepdims=True)
    cen = acc - mu
    var = jnp.mean(cen * cen, axis=-1, keepdims=True)
    y = cen * lax.rsqrt(var + EPS) * lng_ref[...] + lnb_ref[...]
    y = y * jax.nn.sigmoid(y)
    return jnp.dot(y.astype(bf16), pw_ref[...], preferred_element_type=f32)


def _multiscale_pool(u, ubuf, poolw_ref, pscale_ref, t_in_seq):
    tm = u.shape[0]
    ubuf[HALO:HALO + tm, :] = u
    e = ubuf[HALO - POOL_HALO:HALO + tm, :]
    sums = {}
    w = 1
    while w < max(POOL_WINDOWS):
        e = e + pltpu.roll(e, w, 0)
        w *= 2
        sums[w] = e[POOL_HALO:, :]
    ubuf[0:HALO, :] = ubuf[tm:tm + HALO, :]
    lane = lax.broadcasted_iota(jnp.int32, (tm, D_POOL), 1)
    row = lax.broadcasted_iota(jnp.int32, (tm, D_POOL), 0)
    pos = (row + t_in_seq * tm + 1).astype(f32)
    box = sums[POOL_WINDOWS[-1]]
    win = jnp.full((tm, D_POOL), float(POOL_WINDOWS[-1]), f32)
    for gi in range(len(POOL_WINDOWS) - 2, -1, -1):
        sel = lane < (gi + 1) * POOL_GROUP
        box = jnp.where(sel, sums[POOL_WINDOWS[gi]], box)
        win = jnp.where(sel, float(POOL_WINDOWS[gi]), win)
    d = box / jnp.minimum(pos, win) - u
    yp = jnp.dot(d.astype(bf16), poolw_ref[...], preferred_element_type=f32)
    return yp * pscale_ref[...]


def _in_kernel(x_ref, g1_ref, w_ref, bf_ref, gq_ref, gk_ref, grp_ref, tri_ref, eqt_ref, ek_ref,
               dww_ref, dwb_ref, lng_ref, lnb_ref, pw_ref, poolw_ref, pscale_ref,
               qt_ref, kk_ref, vt_ref, conv_ref, pool_ref, carry_ref, hbuf, ubuf, *, tiles_per_seq):
    tm = x_ref.shape[0]
    t_in_seq = pl.program_id(0) % tiles_per_seq

    @pl.when(t_in_seq == 0)
    def _():
        carry_ref[...] = jnp.zeros_like(carry_ref)
        hbuf[0:HALO, :] = jnp.zeros((HALO, D_CONV), f32)
        ubuf[0:HALO, :] = jnp.zeros((HALO, D_POOL), f32)

    x = x_ref[...]
    ms = jnp.mean(x * x, axis=-1, keepdims=True)
    h = (x * lax.rsqrt(ms + EPS) * g1_ref[...]).astype(bf16)
    proj_m = jnp.dot(h, w_ref[:, :N_MIX_IN], preferred_element_type=f32)
    proj_a = jnp.dot(h, w_ref[:, N_MIX_IN:], preferred_element_type=f32)
    cv = proj_m[:, 0:2 * D_CONV]
    pin = proj_m[:, 2 * D_CONV:2 * D_CONV + D_POOL]
    fg = proj_m[:, 2 * D_CONV + D_POOL:N_MIX_IN]
    q = proj_a[:, 0:D_ATT]
    k = proj_a[:, D_ATT:2 * D_ATT]
    v = proj_a[:, 2 * D_ATT:3 * D_ATT]

    grp = grp_ref[...]

    def head_norm(t, g_ref):
        tt = t * t
        hi = tt.astype(bf16)
        lo = (tt - hi.astype(f32)).astype(bf16)
        halves = []
        for c in range(D_ATT // 256):
            sl = slice(c * 256, (c + 1) * 256)
            halves.append(jnp.dot(hi[:, sl], grp, preferred_element_type=f32)
                          + jnp.dot(lo[:, sl], grp, preferred_element_type=f32))
        msq = jnp.concatenate(halves, axis=1)
        return t * lax.rsqrt(msq + EPS) * g_ref[...]

    qn = head_norm(q, gq_ref)
    kn = head_norm(k, gk_ref).astype(bf16)

    lane = lax.broadcasted_iota(jnp.int32, (tm, FG_LANES), 1)
    lf = jax.nn.log_sigmoid(fg + bf_ref[...]) * LOG2E
    l0, l1, l2 = _split3(lf)
    packed = _pack_groups(lane, l0, l1, l2, 0.0).astype(bf16)
    part = jnp.dot(tri_ref[...], packed, preferred_element_type=f32)
    g = ATT_HEADS
    tot = (part + pltpu.roll(part, g, 1) + pltpu.roll(part, 2 * g, 1)
           + pltpu.roll(part, FG_LANES - g, 1) + pltpu.roll(part, FG_LANES - 2 * g, 1))
    cum = tot + carry_ref[0:1, :]
    carry_ref[...] = jnp.broadcast_to(cum[tm - 1:tm, :], carry_ref.shape)

    c0, c1, c2 = _split3(cum)
    ones_lane = jnp.where(lane == _SPLITS * ATT_HEADS, 1.0, 0.0)
    aug_f32 = _pack_groups(lane, c0, c1, c2, ones_lane)
    aug_src = aug_f32.astype(bf16)
    ka = jnp.dot(aug_src, ek_ref[...], preferred_element_type=f32).astype(bf16)
    qat = jnp.dot(eqt_ref[...], aug_f32.T.astype(bf16),
                  preferred_element_type=f32).astype(bf16)

    pieces = []
    for p in range(N_PAIRS):
        sl = slice(p * LANES, (p + 1) * LANES)
        pieces += [kn[:, sl], ka[:, sl]]
    kk_ref[...] = jnp.concatenate(pieces, axis=1)

    qnt = qn.T.astype(bf16)
    vt = v.T.astype(bf16)
    for p in range(N_PAIRS):
        sl = slice(p * LANES, (p + 1) * LANES)
        qt_ref[0, p, 0] = jnp.concatenate([qnt[sl, :], qat[sl, :]], axis=0)
        vt_ref[0, p, 0] = vt[sl, :]

    conv_ref[...] = _conformer_conv(cv, hbuf, dww_ref, dwb_ref, lng_ref, lnb_ref,
                                    pw_ref).astype(bf16)
    pool_ref[...] = _multiscale_pool(pin, ubuf, poolw_ref, pscale_ref, t_in_seq).astype(bf16)


def _in_proj_mix(x2, g1, w_cat, bf_pad, gq, gk, consts, mix_params, batch, seq):
    t = x2.shape[0]
    tm = ROW_TILE
    tiles_per_seq = seq // tm
    n_tiles = t // tm
    const = lambda shape: pl.BlockSpec(shape, lambda i: (0,) * len(shape))
    return pl.pallas_call(
        functools.partial(_in_kernel, tiles_per_seq=tiles_per_seq),
        grid=(n_tiles,),
        in_specs=[
            pl.BlockSpec((tm, D_MODEL), lambda i: (i, 0)),
            const((1, D_MODEL)),
            const((D_MODEL, N_PROJ)),
            const((1, FG_LANES)),
            const((1, D_ATT)),
            const((1, D_ATT)),
            const((256, 256)),
            const((tm, tm)),
            const((N_PAIRS * LANES, LANES)),
            const((LANES, N_PAIRS * LANES)),
            const((HALO, D_CONV)),
            const((1, D_CONV)),
            const((1, D_CONV)),
            const((1, D_CONV)),
            const((D_CONV, D_CONV)),
            const((D_POOL, D_POOL)),
            const((1, D_POOL)),
        ],
        out_specs=[
            pl.BlockSpec((1, N_PAIRS, 1, 2 * LANES, tm),
                         lambda i: (i // tiles_per_seq, 0, i % tiles_per_seq, 0, 0)),
            pl.BlockSpec((tm, 2 * D_ATT), lambda i: (i, 0)),
            pl.BlockSpec((1, N_PAIRS, 1, LANES, tm),
                         lambda i: (i // tiles_per_seq, 0, i % tiles_per_seq, 0, 0)),
            pl.BlockSpec((tm, D_CONV), lambda i: (i, 0)),
            pl.BlockSpec((tm, D_POOL), lambda i: (i, 0)),
        ],
        out_shape=[
            jax.ShapeDtypeStruct((batch, N_PAIRS, tiles_per_seq, 2 * LANES, tm), bf16),
            jax.ShapeDtypeStruct((t, 2 * D_ATT), bf16),
            jax.ShapeDtypeStruct((batch, N_PAIRS, tiles_per_seq, LANES, tm), bf16),
            jax.ShapeDtypeStruct((t, D_CONV), bf16),
            jax.ShapeDtypeStruct((t, D_POOL), bf16),
        ],
        scratch_shapes=[pltpu.VMEM((SUBLANES, FG_LANES), f32),
                        pltpu.VMEM((HALO + tm, D_CONV), f32),
                        pltpu.VMEM((HALO + tm, D_POOL), f32)],
        compiler_params=pltpu.CompilerParams(
            dimension_semantics=("arbitrary",), vmem_limit_bytes=VMEM_LIMIT),
        name="in_proj_mix",
    )(x2, g1, w_cat, bf_pad, gq, gk, *consts, *mix_params)


def _block_schedule(nq):
    blocks = [(i, j) for i in range(nq) for j in range(i)]
    n_unmasked = len(blocks)
    blocks += [(i, i) for i in range(nq)]
    blocks.append((0, 0))
    bi = np.asarray([b[0] for b in blocks], np.int32)
    bj = np.asarray([b[1] for b in blocks], np.int32)
    return bi, bj, n_unmasked


def _att_kernel(bi_ref, bj_ref, qt_ref, kk_ref, vt_ref, o_ref, qh_ref, s0_ref, s1_ref,
                mt0_ref, mt1_ref, m_ref, acc_ref, *, n_unmasked):
    nq, _, _, tq = qh_ref.shape
    n_blocks = n_unmasked + nq
    feat = lax.broadcasted_iota(jnp.int32, (2 * LANES, 1), 0)
    aug = feat - LANES
    row = lax.broadcasted_iota(jnp.int32, (LANES, 1), 0)

    def in_range(v, lo, hi):
        return (v >= lo) & (v < hi)

    s = _SPLITS
    keep = [
        in_range(feat, 0, HEAD_DIM) | in_range(aug, 0, s) | in_range(aug, 2 * s, 3 * s),
        in_range(feat, HEAD_DIM, LANES) | in_range(aug, s, 2 * s) | in_range(aug, 3 * s, 4 * s),
    ]
    for it in range(nq):
        qt = qt_ref[0, 0, it]
        for hh in range(2):
            qh_ref[it, hh] = jnp.where(keep[hh], qt, jnp.zeros_like(qt))
    vkeep = [row < HEAD_DIM, row >= HEAD_DIM]
    m_ref[...] = jnp.full(m_ref.shape, NEG, f32)
    acc_ref[...] = jnp.zeros(acc_ref.shape, f32)
    s_refs = (s0_ref, s1_ref)
    mt_refs = (mt0_ref, mt1_ref)

    def produce(slot, n, heads):
        i = bi_ref[n]
        off = pl.multiple_of(bj_ref[n] * tq, tq)
        kt = kk_ref[0, pl.ds(off, tq), :]
        for hh in heads:
            st = jnp.dot(kt, qh_ref[i, hh], preferred_element_type=f32)
            s_refs[slot][hh] = st
            mt_refs[slot][hh] = jnp.max(st, axis=0, keepdims=True)

    def consume(slot, n, masked, heads):
        i = bi_ref[n]
        vt = vt_ref[0, 0, bj_ref[n]]
        for hh in heads:
            st = s_refs[slot][hh]
            if masked:
                kidx = lax.broadcasted_iota(jnp.int32, st.shape, 0)
                qidx = lax.broadcasted_iota(jnp.int32, st.shape, 1)
                st = jnp.where(kidx <= qidx, st, NEG)
                mt = jnp.max(st, axis=0, keepdims=True)
            else:
                mt = mt_refs[slot][hh]
            m = m_ref[i, hh]
            m_new = jnp.maximum(m, mt)
            p = jnp.exp2(st - m_new).astype(bf16)
            alpha = jnp.exp2(m - m_new)
            vh = jnp.where(vkeep[hh], vt, jnp.ones_like(vt))
            acc_ref[i, hh] = acc_ref[i, hh] * alpha + jnp.dot(vh, p, preferred_element_type=f32)
            m_ref[i, hh] = m_new

    def step(pslot, pn, cslot, cn, masked):
        for hh in range(2):
            produce(pslot, pn, (hh,))
            consume(cslot, cn, masked, (hh,))

    def run(start, count, masked):
        par = start % 2
        full = count // ATT_UNROLL
        if full:
            def body(t, c):
                base = start + t * ATT_UNROLL
                for u in range(ATT_UNROLL):
                    step((u + 1 + par) % 2, base + u + 1, (u + par) % 2, base + u, masked)
                return c
            lax.fori_loop(0, full, body, 0)
        for n in range(start + full * ATT_UNROLL, start + count):
            step((n + 1) % 2, n + 1, n % 2, n, masked)

    produce(0, 0, (0, 1))
    run(0, n_unmasked, False)
    run(n_unmasked, n_blocks - n_unmasked, True)

    for it in range(nq):
        acc0 = acc_ref[it, 0]
        acc1 = acc_ref[it, 1]
        out0 = acc0 / acc0[HEAD_DIM:HEAD_DIM + 1, :]
        out1 = acc1 / acc1[0:1, :]
        ot = jnp.where(row < HEAD_DIM, out0, out1)
        o_ref[0, it * tq:(it + 1) * tq, :] = ot.T.astype(bf16)


def _attention(qq, kk, vt, batch, seq):
    tq = ROW_TILE
    nq = seq // tq
    bi, bj, n_unmasked = _block_schedule(nq)
    grid_spec = pltpu.PrefetchScalarGridSpec(
        num_scalar_prefetch=2,
        grid=(batch, N_PAIRS),
        in_specs=[
            pl.BlockSpec((1, 1, nq, 2 * LANES, tq), lambda b, p, bi, bj: (b, p, 0, 0, 0)),
            pl.BlockSpec((1, seq, 2 * LANES), lambda b, p, bi, bj: (b, 0, p)),
            pl.BlockSpec((1, 1, nq, LANES, tq), lambda b, p, bi, bj: (b, p, 0, 0, 0)),
        ],
        out_specs=pl.BlockSpec((1, seq, LANES), lambda b, p, bi, bj: (b, 0, p)),
        scratch_shapes=[pltpu.VMEM((nq, 2, 2 * LANES, tq), bf16),
                        pltpu.VMEM((2, tq, tq), f32), pltpu.VMEM((2, tq, tq), f32),
                        pltpu.VMEM((2, 1, tq), f32), pltpu.VMEM((2, 1, tq), f32),
                        pltpu.VMEM((nq, 2, 1, tq), f32), pltpu.VMEM((nq, 2, LANES, tq), f32)],
    )
    return pl.pallas_call(
        functools.partial(_att_kernel, n_unmasked=n_unmasked),
        grid_spec=grid_spec,
        out_shape=jax.ShapeDtypeStruct((batch, seq, D_ATT), bf16),
        compiler_params=pltpu.CompilerParams(
            dimension_semantics=("arbitrary", "arbitrary"), vmem_limit_bytes=VMEM_LIMIT),
        name="fox_attention",
    )(jnp.asarray(bi), jnp.asarray(bj), qq, kk, vt)


def _ffn_kernel(x_ref, att_ref, conv_ref, pool_ref, wo_ref, g2_ref, wup_ref, dw_ref, wdn_ref,
                out_ref, tail_ref, act_ref, *, tiles_per_seq):
    tm = x_ref.shape[0]

    @pl.when(pl.program_id(0) % tiles_per_seq == 0)
    def _():
        tail_ref[...] = jnp.zeros_like(tail_ref)

    mix = jnp.concatenate([att_ref[...], conv_ref[...], pool_ref[...]], axis=1)
    x1 = x_ref[...] + jnp.dot(mix, wo_ref[...], preferred_element_type=f32)
    ms = jnp.mean(x1 * x1, axis=-1, keepdims=True)
    h2 = (x1 * lax.rsqrt(ms + EPS) * g2_ref[...]).astype(bf16)

    def up_conv(col):
        cols = slice(col, col + FF_CHUNK)
        up = jnp.dot(h2, wup_ref[:, cols], preferred_element_type=f32)
        ext = jnp.concatenate([tail_ref[:, cols], up], axis=0)
        tail_ref[:, cols] = up[tm - SUBLANES:tm, :]
        s1 = pltpu.roll(ext, 1, 0)[SUBLANES:, :]
        s2 = pltpu.roll(ext, 2, 0)[SUBLANES:, :]
        w = dw_ref[:, cols]
        return w[0:1, :] * s2 + w[1:2, :] * s1 + w[2:3, :] * up

    for c in range(N_FF_CHUNKS):
        gate = up_conv(c * FF_CHUNK)
        val = up_conv(D_FF + c * FF_CHUNK)
        act_ref[:, c * FF_CHUNK:(c + 1) * FF_CHUNK] = (
            gate * jax.nn.sigmoid(gate) * val).astype(bf16)

    out_ref[...] = x1 + jnp.dot(act_ref[...], wdn_ref[...], preferred_element_type=f32)


def _ffn(x2, att, conv, pool, wo, g2, wup, dw, wdn, seq):
    t = x2.shape[0]
    tm = ROW_TILE
    tiles_per_seq = seq // tm
    resident = lambda shape: pl.BlockSpec(shape, lambda i: (0,) * len(shape),
                                          pipeline_mode=pl.Buffered(1))
    return pl.pallas_call(
        functools.partial(_ffn_kernel, tiles_per_seq=tiles_per_seq),
        grid=(t // tm,),
        in_specs=[
            pl.BlockSpec((tm, D_MODEL), lambda i: (i, 0)),
            pl.BlockSpec((tm, D_ATT), lambda i: (i, 0)),
            pl.BlockSpec((tm, D_CONV), lambda i: (i, 0)),
            pl.BlockSpec((tm, D_POOL), lambda i: (i, 0)),
            resident((D_MODEL, D_MODEL)),
            resident((1, D_MODEL)),
            resident((D_MODEL, 2 * D_FF)),
            resident((SUBLANES, 2 * D_FF)),
            resident((D_FF, D_MODEL)),
        ],
        out_specs=pl.BlockSpec((tm, D_MODEL), lambda i: (i, 0)),
        out_shape=jax.ShapeDtypeStruct((t, D_MODEL), f32),
        scratch_shapes=[pltpu.VMEM((SUBLANES, 2 * D_FF), f32),
                        pltpu.VMEM((tm, D_FF), bf16)],
        compiler_params=pltpu.CompilerParams(
            dimension_semantics=("arbitrary",), vmem_limit_bytes=VMEM_LIMIT),
        name="out_proj_ffn",
    )(x2, att, conv, pool, wo, g2, wup, dw, wdn)


def _constants():
    grp = np.kron(np.eye(256 // HEAD_DIM, dtype=np.float32),
                  np.full((HEAD_DIM, HEAD_DIM), 1.0 / HEAD_DIM, np.float32))
    tri = np.tril(np.ones((ROW_TILE, ROW_TILE), np.float32))
    eq, ek = _placement_matrices()
    return tuple(jnp.asarray(a, bf16) for a in (grp, tri, eq.T, ek))


def kernel(x, norm1_g, w_in, b_f, q_norm_g, k_norm_g, conv_dw_w, conv_dw_b, conv_ln_g, conv_ln_b,
           conv_pw_w, pool_w, pool_scale, w_out, norm2_g, w_up, ffn_dw_w, w_down):
    batch, seq, _ = x.shape
    depth = w_in.shape[0]
    assert seq % ROW_TILE == 0
    consts = _constants()
    x2 = x.reshape(batch * seq, D_MODEL)

    for l in range(depth):
        wi = w_in[l]
        o = 3 * D_ATT
        w_fg = wi[:, o:o + ATT_HEADS]
        o += ATT_HEADS
        w_cv = wi[:, o:o + 2 * D_CONV]
        o += 2 * D_CONV
        w_pl = wi[:, o:o + D_POOL]
        fg_pad = jnp.zeros((D_MODEL, FG_LANES - _SPLITS * ATT_HEADS), f32)
        w_cat = jnp.concatenate([w_cv, w_pl] + [w_fg] * _SPLITS + [fg_pad, wi[:, :3 * D_ATT]],
                                axis=1).astype(bf16)
        bf_pad = jnp.concatenate([b_f[l]] * _SPLITS
                                 + [jnp.zeros((FG_LANES - _SPLITS * ATT_HEADS,), f32)])[None, :]
        gq = jnp.tile(q_norm_g[l] * (ATT_SCALE * LOG2E), ATT_HEADS)[None, :]
        gk = jnp.tile(k_norm_g[l], ATT_HEADS)[None, :]
        dww = jnp.concatenate([conv_dw_w[l], jnp.zeros((HALO - CONV_WIDTH, D_CONV), f32)], axis=0)
        poolw = jax.scipy.linalg.block_diag(*[pool_w[l, g] for g in range(len(POOL_WINDOWS))])
        mix_params = (dww, conv_dw_b[l][None, :], conv_ln_g[l][None, :], conv_ln_b[l][None, :],
                      conv_pw_w[l].astype(bf16), poolw.astype(bf16), pool_scale[l][None, :])

        qt, kk, vt, conv, pool = _in_proj_mix(x2, norm1_g[l][None, :], w_cat, bf_pad, gq, gk,
                                              consts, mix_params, batch, seq)

        att = _attention(qt, kk.reshape(batch, seq, 2 * D_ATT),
                         vt, batch, seq).reshape(batch * seq, D_ATT)

        dw = jnp.concatenate(
            [ffn_dw_w[l], jnp.zeros((SUBLANES - FFN_CONV_WIDTH, 2 * D_FF), f32)], axis=0)
        x2 = _ffn(x2, att, conv, pool, w_out[l].astype(bf16), norm2_g[l][None, :],
                  w_up[l].astype(bf16), dw, w_down[l].astype(bf16), seq)

    return x2.reshape(batch, seq, D_MODEL)
```

```python
import functools

import numpy as np
import jax
import jax.numpy as jnp
from jax import lax
from jax.experimental import pallas as pl
from jax.experimental.pallas import tpu as pltpu

D_MODEL = 1024
ATT_HEADS = 8
HEAD_DIM = 64
D_ATT = ATT_HEADS * HEAD_DIM
N_PAIRS = ATT_HEADS // 2
ATT_SCALE = HEAD_DIM ** -0.5
D_CONV = 256
CONV_WIDTH = 31
POOL_WINDOWS = (2, 4, 8, 16)
POOL_GROUP = 64
D_POOL = len(POOL_WINDOWS) * POOL_GROUP
D_FF = 2816
FFN_CONV_WIDTH = 3
EPS = 1e-6
NEG = -1e30
LOG2E = 1.4426950408889634

LANES = 128
SUBLANES = 8
FG_LANES = LANES
N_MIX_IN = 2 * D_CONV + D_POOL + FG_LANES
N_PROJ = N_MIX_IN + 3 * D_ATT
ROW_TILE = 512
HALO = 32
POOL_HALO = 16
FF_CHUNK = 256
N_FF_CHUNKS = D_FF // FF_CHUNK
ATT_UNROLL = 8
VMEM_LIMIT = 56 * 1024 * 1024

f32 = jnp.float32
bf16 = jnp.bfloat16

_SPLITS = 3


def _placement_matrices():
    eq = np.zeros((LANES, N_PAIRS * LANES), np.float32)
    ek = np.zeros((LANES, N_PAIRS * LANES), np.float32)
    for h in range(ATT_HEADS):
        p, hh = divmod(h, 2)
        for s in range(_SPLITS):
            src = s * ATT_HEADS + h
            ek[src, p * LANES + hh * _SPLITS + s] = -1.0
            eq[src, p * LANES + 2 * _SPLITS + hh * _SPLITS + s] = 1.0
    one_lane = _SPLITS * ATT_HEADS
    for p in range(N_PAIRS):
        ek[one_lane, p * LANES + 2 * _SPLITS:p * LANES + 4 * _SPLITS] = 1.0
        eq[one_lane, p * LANES:p * LANES + 2 * _SPLITS] = 1.0
    return eq, ek


def _split3(c):
    hi = c.astype(bf16).astype(f32)
    r1 = c - hi
    mid = r1.astype(bf16).astype(f32)
    lo = r1 - mid
    return hi, mid, lo


def _pack_groups(lane, a0, a1, a2, rest):
    return jnp.where(lane < ATT_HEADS, a0,
                     jnp.where(lane < 2 * ATT_HEADS, a1,
                               jnp.where(lane < 3 * ATT_HEADS, a2, rest)))


def _conformer_conv(cv, hbuf, dww_ref, dwb_ref, lng_ref, lnb_ref, pw_ref):
    tm = cv.shape[0]
    hbuf[HALO:HALO + tm, :] = cv[:, :D_CONV] * jax.nn.sigmoid(cv[:, D_CONV:])
    ext = tm + SUBLANES
    acc = None
    for r in range(SUBLANES):
        part = None
        for a in range((CONV_WIDTH - 1 - r) // SUBLANES + 1):
            tap = CONV_WIDTH - 1 - (SUBLANES * a + r)
            lo = HALO - SUBLANES - SUBLANES * a
            term = dww_ref[tap:tap + 1, :] * hbuf[lo:lo + ext, :]
            part = term if part is None else part + term
        if r:
            part = pltpu.roll(part, r, 0)
        part = part[SUBLANES:, :]
        acc = part if acc is None else acc + part
    acc = acc + dwb_ref[...]
    hbuf[0:HALO, :] = hbuf[tm:tm + HALO, :]
    mu = jnp.mean(acc, axis=-1, keepdims=True)
    cen = acc - mu
    var = jnp.mean(cen * cen, axis=-1, keepdims=True)
    y = cen * lax.rsqrt(var + EPS) * lng_ref[...] + lnb_ref[...]
    y = y * jax.nn.sigmoid(y)
    return jnp.dot(y.astype(bf16), pw_ref[...], preferred_element_type=f32)


def _multiscale_pool(u, ubuf, poolw_ref, pscale_ref, t_in_seq):
    tm = u.shape[0]
    ubuf[HALO:HALO + tm, :] = u
    e = ubuf[HALO - POOL_HALO:HALO + tm, :]
    sums = {}
    w = 1
    while w < max(POOL_WINDOWS):
        e = e + pltpu.roll(e, w, 0)
        w *= 2
        sums[w] = e[POOL_HALO:, :]
    ubuf[0:HALO, :] = ubuf[tm:tm + HALO, :]
    lane = lax.broadcasted_iota(jnp.int32, (tm, D_POOL), 1)
    row = lax.broadcasted_iota(jnp.int32, (tm, D_POOL), 0)
    pos = (row + t_in_seq * tm + 1).astype(f32)
    box = sums[POOL_WINDOWS[-1]]
    win = jnp.full((tm, D_POOL), float(POOL_WINDOWS[-1]), f32)
    for gi in range(len(POOL_WINDOWS) - 2, -1, -1):
        sel = lane < (gi + 1) * POOL_GROUP
        box = jnp.where(sel, sums[POOL_WINDOWS[gi]], box)
        win = jnp.where(sel, float(POOL_WINDOWS[gi]), win)
    d = box / jnp.minimum(pos, win) - u
    yp = jnp.dot(d.astype(bf16), poolw_ref[...], preferred_element_type=f32)
    return yp * pscale_ref[...]


def _in_kernel(x_ref, g1_ref, w_ref, bf_ref, gq_ref, gk_ref, grp_ref, tri_ref, eqt_ref, ek_ref,
               dww_ref, dwb_ref, lng_ref, lnb_ref, pw_ref, poolw_ref, pscale_ref,
               qt_ref, kk_ref, vt_ref, conv_ref, pool_ref, carry_ref, hbuf, ubuf, *, tiles_per_seq):
    tm = x_ref.shape[0]
    t_in_seq = pl.program_id(0) % tiles_per_seq

    @pl.when(t_in_seq == 0)
    def _():
        carry_ref[...] = jnp.zeros_like(carry_ref)
        hbuf[0:HALO, :] = jnp.zeros((HALO, D_CONV), f32)
        ubuf[0:HALO, :] = jnp.zeros((HALO, D_POOL), f32)

    x = x_ref[...]
    ms = jnp.mean(x * x, axis=-1, keepdims=True)
    h = (x * lax.rsqrt(ms + EPS) * g1_ref[...]).astype(bf16)
    proj_m = jnp.dot(h, w_ref[:, :N_MIX_IN], preferred_element_type=f32)
    proj_a = jnp.dot(h, w_ref[:, N_MIX_IN:], preferred_element_type=f32)
    cv = proj_m[:, 0:2 * D_CONV]
    pin = proj_m[:, 2 * D_CONV:2 * D_CONV + D_POOL]
    fg = proj_m[:, 2 * D_CONV + D_POOL:N_MIX_IN]
    q = proj_a[:, 0:D_ATT]
    k = proj_a[:, D_ATT:2 * D_ATT]
    v = proj_a[:, 2 * D_ATT:3 * D_ATT]

    grp = grp_ref[...]

    def head_norm(t, g_ref):
        tt = (t * t).astype(bf16)
        halves = [jnp.dot(tt[:, c * 256:(c + 1) * 256], grp, preferred_element_type=f32)
                  for c in range(D_ATT // 256)]
        msq = jnp.concatenate(halves, axis=1)
        return t * lax.rsqrt(msq + EPS) * g_ref[...]

    qn = head_norm(q, gq_ref)
    kn = head_norm(k, gk_ref).astype(bf16)

    lane = lax.broadcasted_iota(jnp.int32, (tm, FG_LANES), 1)
    lf = jax.nn.log_sigmoid(fg + bf_ref[...]) * LOG2E
    l0, l1, l2 = _split3(lf)
    packed = _pack_groups(lane, l0, l1, l2, 0.0).astype(bf16)
    part = jnp.dot(tri_ref[...], packed, preferred_element_type=f32)
    g = ATT_HEADS
    tot = (part + pltpu.roll(part, g, 1) + pltpu.roll(part, 2 * g, 1)
           + pltpu.roll(part, FG_LANES - g, 1) + pltpu.roll(part, FG_LANES - 2 * g, 1))
    cum = tot + carry_ref[0:1, :]
    carry_ref[...] = jnp.broadcast_to(cum[tm - 1:tm, :], carry_ref.shape)

    c0, c1, c2 = _split3(cum)
    ones_lane = jnp.where(lane == _SPLITS * ATT_HEADS, 1.0, 0.0)
    aug_f32 = _pack_groups(lane, c0, c1, c2, ones_lane)
    aug_src = aug_f32.astype(bf16)
    ka = jnp.dot(aug_src, ek_ref[...], preferred_element_type=f32).astype(bf16)
    qat = jnp.dot(eqt_ref[...], aug_f32.T.astype(bf16),
                  preferred_element_type=f32).astype(bf16)

    pieces = []
    for p in range(N_PAIRS):
        sl = slice(p * LANES, (p + 1) * LANES)
        pieces += [kn[:, sl], ka[:, sl]]
    kk_ref[...] = jnp.concatenate(pieces, axis=1)

    qnt = qn.T.astype(bf16)
    vt = v.T.astype(bf16)
    for p in range(N_PAIRS):
        sl = slice(p * LANES, (p + 1) * LANES)
        qt_ref[0, p, 0] = jnp.concatenate([qnt[sl, :], qat[sl, :]], axis=0)
        vt_ref[0, p, 0] = vt[sl, :]

    conv_ref[...] = _conformer_conv(cv, hbuf, dww_ref, dwb_ref, lng_ref, lnb_ref,
                                    pw_ref).astype(bf16)
    pool_ref[...] = _multiscale_pool(pin, ubuf, poolw_ref, pscale_ref, t_in_seq).astype(bf16)


def _in_proj_mix(x2, g1, w_cat, bf_pad, gq, gk, consts, mix_params, batch, seq):
    t = x2.shape[0]
    tm = ROW_TILE
    tiles_per_seq = seq // tm
    n_tiles = t // tm
    const = lambda shape: pl.BlockSpec(shape, lambda i: (0,) * len(shape))
    return pl.pallas_call(
        functools.partial(_in_kernel, tiles_per_seq=tiles_per_seq),
        grid=(n_tiles,),
        in_specs=[
            pl.BlockSpec((tm, D_MODEL), lambda i: (i, 0)),
            const((1, D_MODEL)),
            const((D_MODEL, N_PROJ)),
            const((1, FG_LANES)),
            const((1, D_ATT)),
            const((1, D_ATT)),
            const((256, 256)),
            const((tm, tm)),
            const((N_PAIRS * LANES, LANES)),
            const((LANES, N_PAIRS * LANES)),
            const((HALO, D_CONV)),
            const((1, D_CONV)),
            const((1, D_CONV)),
            const((1, D_CONV)),
            const((D_CONV, D_CONV)),
            const((D_POOL, D_POOL)),
            const((1, D_POOL)),
        ],
        out_specs=[
            pl.BlockSpec((1, N_PAIRS, 1, 2 * LANES, tm),
                         lambda i: (i // tiles_per_seq, 0, i % tiles_per_seq, 0, 0)),
            pl.BlockSpec((tm, 2 * D_ATT), lambda i: (i, 0)),
            pl.BlockSpec((1, N_PAIRS, 1, LANES, tm),
                         lambda i: (i // tiles_per_seq, 0, i % tiles_per_seq, 0, 0)),
            pl.BlockSpec((tm, D_CONV), lambda i: (i, 0)),
            pl.BlockSpec((tm, D_POOL), lambda i: (i, 0)),
        ],
        out_shape=[
            jax.ShapeDtypeStruct((batch, N_PAIRS, tiles_per_seq, 2 * LANES, tm), bf16),
            jax.ShapeDtypeStruct((t, 2 * D_ATT), bf16),
            jax.ShapeDtypeStruct((batch, N_PAIRS, tiles_per_seq, LANES, tm), bf16),
            jax.ShapeDtypeStruct((t, D_CONV), bf16),
            jax.ShapeDtypeStruct((t, D_POOL), bf16),
        ],
        scratch_shapes=[pltpu.VMEM((SUBLANES, FG_LANES), f32),
                        pltpu.VMEM((HALO + tm, D_CONV), f32),
                        pltpu.VMEM((HALO + tm, D_POOL), f32)],
        compiler_params=pltpu.CompilerParams(
            dimension_semantics=("arbitrary",), vmem_limit_bytes=VMEM_LIMIT),
        name="in_proj_mix",
    )(x2, g1, w_cat, bf_pad, gq, gk, *consts, *mix_params)


def _block_schedule(nq):
    blocks = [(i, j) for i in range(nq) for j in range(i)]
    n_unmasked = len(blocks)
    blocks += [(i, i) for i in range(nq)]
    blocks.append((0, 0))
    bi = np.asarray([b[0] for b in blocks], np.int32)
    bj = np.asarray([b[1] for b in blocks], np.int32)
    return bi, bj, n_unmasked


def _att_kernel(bi_ref, bj_ref, qt_ref, kk_ref, vt_ref, o_ref, qh_ref, s0_ref, s1_ref,
                mt0_ref, mt1_ref, m_ref, acc_ref, *, n_unmasked):
    nq, _, _, tq = qh_ref.shape
    n_blocks = n_unmasked + nq
    feat = lax.broadcasted_iota(jnp.int32, (2 * LANES, 1), 0)
    aug = feat - LANES
    row = lax.broadcasted_iota(jnp.int32, (LANES, 1), 0)

    def in_range(v, lo, hi):
        return (v >= lo) & (v < hi)

    s = _SPLITS
    keep = [
        in_range(feat, 0, HEAD_DIM) | in_range(aug, 0, s) | in_range(aug, 2 * s, 3 * s),
        in_range(feat, HEAD_DIM, LANES) | in_range(aug, s, 2 * s) | in_range(aug, 3 * s, 4 * s),
    ]
    for it in range(nq):
        qt = qt_ref[0, 0, it]
        for hh in range(2):
            qh_ref[it, hh] = jnp.where(keep[hh], qt, jnp.zeros_like(qt))
    vkeep = [row < HEAD_DIM, row >= HEAD_DIM]
    m_ref[...] = jnp.full(m_ref.shape, NEG, f32)
    acc_ref[...] = jnp.zeros(acc_ref.shape, f32)
    s_refs = (s0_ref, s1_ref)
    mt_refs = (mt0_ref, mt1_ref)

    def produce(slot, n, heads):
        i = bi_ref[n]
        off = pl.multiple_of(bj_ref[n] * tq, tq)
        kt = kk_ref[0, pl.ds(off, tq), :]
        for hh in heads:
            st = jnp.dot(kt, qh_ref[i, hh], preferred_element_type=f32)
            s_refs[slot][hh] = st
            mt_refs[slot][hh] = jnp.max(st, axis=0, keepdims=True)

    def consume(slot, n, masked, heads):
        i = bi_ref[n]
        vt = vt_ref[0, 0, bj_ref[n]]
        for hh in heads:
            st = s_refs[slot][hh]
            if masked:
                kidx = lax.broadcasted_iota(jnp.int32, st.shape, 0)
                qidx = lax.broadcasted_iota(jnp.int32, st.shape, 1)
                st = jnp.where(kidx <= qidx, st, NEG)
                mt = jnp.max(st, axis=0, keepdims=True)
            else:
                mt = mt_refs[slot][hh]
            m = m_ref[i, hh]
            m_new = jnp.maximum(m, mt)
            p = jnp.exp2(st - m_new).astype(bf16)
            alpha = jnp.exp2(m - m_new)
            vh = jnp.where(vkeep[hh], vt, jnp.ones_like(vt))
            acc_ref[i, hh] = acc_ref[i, hh] * alpha + jnp.dot(vh, p, preferred_element_type=f32)
            m_ref[i, hh] = m_new

    def step(pslot, pn, cslot, cn, masked):
        for hh in range(2):
            produce(pslot, pn, (hh,))
            consume(cslot, cn, masked, (hh,))

    def run(start, count, masked):
        par = start % 2
        full = count // ATT_UNROLL
        if full:
            def body(t, c):
                base = start + t * ATT_UNROLL
                for u in range(ATT_UNROLL):
                    step((u + 1 + par) % 2, base + u + 1, (u + par) % 2, base + u, masked)
                return c
            lax.fori_loop(0, full, body, 0)
        for n in range(start + full * ATT_UNROLL, start + count):
            step((n + 1) % 2, n + 1, n % 2, n, masked)

    produce(0, 0, (0, 1))
    run(0, n_unmasked, False)
    run(n_unmasked, n_blocks - n_unmasked, True)

    for it in range(nq):
        acc0 = acc_ref[it, 0]
        acc1 = acc_ref[it, 1]
        out0 = acc0 / acc0[HEAD_DIM:HEAD_DIM + 1, :]
        out1 = acc1 / acc1[0:1, :]
        ot = jnp.where(row < HEAD_DIM, out0, out1)
        o_ref[0, it * tq:(it + 1) * tq, :] = ot.T.astype(bf16)


def _attention(qq, kk, vt, batch, seq):
    tq = ROW_TILE
    nq = seq // tq
    bi, bj, n_unmasked = _block_schedule(nq)
    grid_spec = pltpu.PrefetchScalarGridSpec(
        num_scalar_prefetch=2,
        grid=(batch, N_PAIRS),
        in_specs=[
            pl.BlockSpec((1, 1, nq, 2 * LANES, tq), lambda b, p, bi, bj: (b, p, 0, 0, 0)),
            pl.BlockSpec((1, seq, 2 * LANES), lambda b, p, bi, bj: (b, 0, p)),
            pl.BlockSpec((1, 1, nq, LANES, tq), lambda b, p, bi, bj: (b, p, 0, 0, 0)),
        ],
        out_specs=pl.BlockSpec((1, seq, LANES), lambda b, p, bi, bj: (b, 0, p)),
        scratch_shapes=[pltpu.VMEM((nq, 2, 2 * LANES, tq), bf16),
                        pltpu.VMEM((2, tq, tq), f32), pltpu.VMEM((2, tq, tq), f32),
                        pltpu.VMEM((2, 1, tq), f32), pltpu.VMEM((2, 1, tq), f32),
                        pltpu.VMEM((nq, 2, 1, tq), f32), pltpu.VMEM((nq, 2, LANES, tq), f32)],
    )
    return pl.pallas_call(
        functools.partial(_att_kernel, n_unmasked=n_unmasked),
        grid_spec=grid_spec,
        out_shape=jax.ShapeDtypeStruct((batch, seq, D_ATT), bf16),
        compiler_params=pltpu.CompilerParams(
            dimension_semantics=("arbitrary", "arbitrary"), vmem_limit_bytes=VMEM_LIMIT),
        name="fox_attention",
    )(jnp.asarray(bi), jnp.asarray(bj), qq, kk, vt)


def _ffn_kernel(x_ref, att_ref, conv_ref, pool_ref, wo_ref, g2_ref, wup_ref, dw_ref, wdn_ref,
                out_ref, tail_ref, act_ref, *, tiles_per_seq):
    tm = x_ref.shape[0]

    @pl.when(pl.program_id(0) % tiles_per_seq == 0)
    def _():
        tail_ref[...] = jnp.zeros_like(tail_ref)

    mix = jnp.concatenate([att_ref[...], conv_ref[...], pool_ref[...]], axis=1)
    x1 = x_ref[...] + jnp.dot(mix, wo_ref[...], preferred_element_type=f32)
    ms = jnp.mean(x1 * x1, axis=-1, keepdims=True)
    h2 = (x1 * lax.rsqrt(ms + EPS) * g2_ref[...]).astype(bf16)

    def up_conv(col):
        cols = slice(col, col + FF_CHUNK)
        up = jnp.dot(h2, wup_ref[:, cols], preferred_element_type=f32)
        ext = jnp.concatenate([tail_ref[:, cols], up], axis=0)
        tail_ref[:, cols] = up[tm - SUBLANES:tm, :]
        s1 = pltpu.roll(ext, 1, 0)[SUBLANES:, :]
        s2 = pltpu.roll(ext, 2, 0)[SUBLANES:, :]
        w = dw_ref[:, cols]
        return w[0:1, :] * s2 + w[1:2, :] * s1 + w[2:3, :] * up

    for c in range(N_FF_CHUNKS):
        gate = up_conv(c * FF_CHUNK)
        val = up_conv(D_FF + c * FF_CHUNK)
        act_ref[:, c * FF_CHUNK:(c + 1) * FF_CHUNK] = (
            gate * jax.nn.sigmoid(gate) * val).astype(bf16)

    out_ref[...] = x1 + jnp.dot(act_ref[...], wdn_ref[...], preferred_element_type=f32)


def _ffn(x2, att, conv, pool, wo, g2, wup, dw, wdn, seq):
    t = x2.shape[0]
    tm = ROW_TILE
    tiles_per_seq = seq // tm
    resident = lambda shape: pl.BlockSpec(shape, lambda i: (0,) * len(shape),
                                          pipeline_mode=pl.Buffered(1))
    return pl.pallas_call(
        functools.partial(_ffn_kernel, tiles_per_seq=tiles_per_seq),
        grid=(t // tm,),
        in_specs=[
            pl.BlockSpec((tm, D_MODEL), lambda i: (i, 0)),
            pl.BlockSpec((tm, D_ATT), lambda i: (i, 0)),
            pl.BlockSpec((tm, D_CONV), lambda i: (i, 0)),
            pl.BlockSpec((tm, D_POOL), lambda i: (i, 0)),
            resident((D_MODEL, D_MODEL)),
            resident((1, D_MODEL)),
            resident((D_MODEL, 2 * D_FF)),
            resident((SUBLANES, 2 * D_FF)),
            resident((D_FF, D_MODEL)),
        ],
        out_specs=pl.BlockSpec((tm, D_MODEL), lambda i: (i, 0)),
        out_shape=jax.ShapeDtypeStruct((t, D_MODEL), f32),
        scratch_shapes=[pltpu.VMEM((SUBLANES, 2 * D_FF), f32),
                        pltpu.VMEM((tm, D_FF), bf16)],
        compiler_params=pltpu.CompilerParams(
            dimension_semantics=("arbitrary",), vmem_limit_bytes=VMEM_LIMIT),
        name="out_proj_ffn",
    )(x2, att, conv, pool, wo, g2, wup, dw, wdn)


def _constants():
    grp = np.kron(np.eye(256 // HEAD_DIM, dtype=np.float32),
                  np.full((HEAD_DIM, HEAD_DIM), 1.0 / HEAD_DIM, np.float32))
    tri = np.tril(np.ones((ROW_TILE, ROW_TILE), np.float32))
    eq, ek = _placement_matrices()
    return tuple(jnp.asarray(a, bf16) for a in (grp, tri, eq.T, ek))


def kernel(x, norm1_g, w_in, b_f, q_norm_g, k_norm_g, conv_dw_w, conv_dw_b, conv_ln_g, conv_ln_b,
           conv_pw_w, pool_w, pool_scale, w_out, norm2_g, w_up, ffn_dw_w, w_down):
    batch, seq, _ = x.shape
    depth = w_in.shape[0]
    assert seq % ROW_TILE == 0
    consts = _constants()
    x2 = x.reshape(batch * seq, D_MODEL)

    for l in range(depth):
        wi = w_in[l]
        o = 3 * D_ATT
        w_fg = wi[:, o:o + ATT_HEADS]
        o += ATT_HEADS
        w_cv = wi[:, o:o + 2 * D_CONV]
        o += 2 * D_CONV
        w_pl = wi[:, o:o + D_POOL]
        fg_pad = jnp.zeros((D_MODEL, FG_LANES - _SPLITS * ATT_HEADS), f32)
        w_cat = jnp.concatenate([w_cv, w_pl] + [w_fg] * _SPLITS + [fg_pad, wi[:, :3 * D_ATT]],
                                axis=1).astype(bf16)
        bf_pad = jnp.concatenate([b_f[l]] * _SPLITS
                                 + [jnp.zeros((FG_LANES - _SPLITS * ATT_HEADS,), f32)])[None, :]
        gq = jnp.tile(q_norm_g[l] * (ATT_SCALE * LOG2E), ATT_HEADS)[None, :]
        gk = jnp.tile(k_norm_g[l], ATT_HEADS)[None, :]
        dww = jnp.concatenate([conv_dw_w[l], jnp.zeros((HALO - CONV_WIDTH, D_CONV), f32)], axis=0)
        poolw = jax.scipy.linalg.block_diag(*[pool_w[l, g] for g in range(len(POOL_WINDOWS))])
        mix_params = (dww, conv_dw_b[l][None, :], conv_ln_g[l][None, :], conv_ln_b[l][None, :],
                      conv_pw_w[l].astype(bf16), poolw.astype(bf16), pool_scale[l][None, :])

        qt, kk, vt, conv, pool = _in_proj_mix(x2, norm1_g[l][None, :], w_cat, bf_pad, gq, gk,
                                              consts, mix_params, batch, seq)

        att = _attention(qt, kk.reshape(batch, seq, 2 * D_ATT),
                         vt, batch, seq).reshape(batch * seq, D_ATT)

        dw = jnp.concatenate(
            [ffn_dw_w[l], jnp.zeros((SUBLANES - FFN_CONV_WIDTH, 2 * D_FF), f32)], axis=0)
        x2 = _ffn(x2, att, conv, pool, w_out[l].astype(bf16), norm2_g[l][None, :],
                  w_up[l].astype(bf16), dw, w_down[l].astype(bf16), seq)

    return x2.reshape(batch, seq, D_MODEL)
```

```python
import functools

import numpy as np
import jax
import jax.numpy as jnp
from jax import lax
from jax.experimental import pallas as pl
from jax.experimental.pallas import tpu as pltpu

D_MODEL = 1024
ATT_HEADS = 8
HEAD_DIM = 64
D_ATT = ATT_HEADS * HEAD_DIM
N_PAIRS = ATT_HEADS // 2
ATT_SCALE = HEAD_DIM ** -0.5
D_CONV = 256
CONV_WIDTH = 31
POOL_WINDOWS = (2, 4, 8, 16)
POOL_GROUP = 64
D_POOL = len(POOL_WINDOWS) * POOL_GROUP
D_FF = 2816
FFN_CONV_WIDTH = 3
EPS = 1e-6
NEG = -1e30
LOG2E = 1.4426950408889634

LANES = 128
SUBLANES = 8
FG_LANES = LANES
N_MIX_IN = 2 * D_CONV + D_POOL + FG_LANES
N_PROJ = N_MIX_IN + 3 * D_ATT
ROW_TILE = 512
HALO = 32
POOL_HALO = 16
FF_CHUNK = 256
N_FF_CHUNKS = D_FF // FF_CHUNK
ATT_UNROLL = 8
VMEM_LIMIT = 56 * 1024 * 1024

f32 = jnp.float32
bf16 = jnp.bfloat16

_SPLITS = 3


def _placement_matrices():
    eq = np.zeros((LANES, N_PAIRS * LANES), np.float32)
    ek = np.zeros((LANES, N_PAIRS * LANES), np.float32)
    for h in range(ATT_HEADS):
        p, hh = divmod(h, 2)
        for s in range(_SPLITS):
            src = s * ATT_HEADS + h
            ek[src, p * LANES + hh * _SPLITS + s] = -1.0
            eq[src, p * LANES + 2 * _SPLITS + hh * _SPLITS + s] = 1.0
    one_lane = _SPLITS * ATT_HEADS
    for p in range(N_PAIRS):
        ek[one_lane, p * LANES + 2 * _SPLITS:p * LANES + 4 * _SPLITS] = 1.0
        eq[one_lane, p * LANES:p * LANES + 2 * _SPLITS] = 1.0
    return eq, ek


def _split3(c):
    hi = c.astype(bf16).astype(f32)
    r1 = c - hi
    mid = r1.astype(bf16).astype(f32)
    lo = r1 - mid
    return hi, mid, lo


def _pack_groups(lane, a0, a1, a2, rest):
    return jnp.where(lane < ATT_HEADS, a0,
                     jnp.where(lane < 2 * ATT_HEADS, a1,
                               jnp.where(lane < 3 * ATT_HEADS, a2, rest)))


def _conformer_conv(cv, hbuf, dww_ref, dwb_ref, lng_ref, lnb_ref, pw_ref):
    tm = cv.shape[0]
    hbuf[HALO:HALO + tm, :] = cv[:, :D_CONV] * jax.nn.sigmoid(cv[:, D_CONV:])
    ext = tm + SUBLANES
    acc = None
    for r in range(SUBLANES):
        part = None
        for a in range((CONV_WIDTH - 1 - r) // SUBLANES + 1):
            tap = CONV_WIDTH - 1 - (SUBLANES * a + r)
            lo = HALO - SUBLANES - SUBLANES * a
            term = dww_ref[tap:tap + 1, :] * hbuf[lo:lo + ext, :]
            part = term if part is None else part + term
        if r:
            part = pltpu.roll(part, r, 0)
        part = part[SUBLANES:, :]
        acc = part if acc is None else acc + part
    acc = acc + dwb_ref[...]
    hbuf[0:HALO, :] = hbuf[tm:tm + HALO, :]
    mu = jnp.mean(acc, axis=-1, keepdims=True)
    cen = acc - mu
    var = jnp.mean(cen * cen, axis=-1, keepdims=True)
    y = cen * lax.rsqrt(var + EPS) * lng_ref[...] + lnb_ref[...]
    y = y * jax.nn.sigmoid(y)
    return jnp.dot(y.astype(bf16), pw_ref[...], preferred_element_type=f32)


def _multiscale_pool(u, ubuf, poolw_ref, pscale_ref, t_in_seq):
    tm = u.shape[0]
    ubuf[HALO:HALO + tm, :] = u
    e = ubuf[HALO - POOL_HALO:HALO + tm, :]
    sums = {}
    w = 1
    while w < max(POOL_WINDOWS):
        e = e + pltpu.roll(e, w, 0)
        w *= 2
        sums[w] = e[POOL_HALO:, :]
    ubuf[0:HALO, :] = ubuf[tm:tm + HALO, :]
    lane = lax.broadcasted_iota(jnp.int32, (tm, D_POOL), 1)
    row = lax.broadcasted_iota(jnp.int32, (tm, D_POOL), 0)
    pos = (row + t_in_seq * tm + 1).astype(f32)
    box = sums[POOL_WINDOWS[-1]]
    win = jnp.full((tm, D_POOL), float(POOL_WINDOWS[-1]), f32)
    for gi in range(len(POOL_WINDOWS) - 2, -1, -1):
        sel = lane < (gi + 1) * POOL_GROUP
        box = jnp.where(sel, sums[POOL_WINDOWS[gi]], box)
        win = jnp.where(sel, float(POOL_WINDOWS[gi]), win)
    d = box / jnp.minimum(pos, win) - u
    yp = jnp.dot(d.astype(bf16), poolw_ref[...], preferred_element_type=f32)
    return yp * pscale_ref[...]


def _in_kernel(x_ref, g1_ref, w_ref, bf_ref, gq_ref, gk_ref, grp_ref, tri_ref, eqt_ref, ek_ref,
               dww_ref, dwb_ref, lng_ref, lnb_ref, pw_ref, poolw_ref, pscale_ref,
               qt_ref, kk_ref, vt_ref, conv_ref, pool_ref, carry_ref, hbuf, ubuf, *, tiles_per_seq):
    tm = x_ref.shape[0]
    t_in_seq = pl.program_id(0) % tiles_per_seq

    @pl.when(t_in_seq == 0)
    def _():
        carry_ref[...] = jnp.zeros_like(carry_ref)
        hbuf[0:HALO, :] = jnp.zeros((HALO, D_CONV), f32)
        ubuf[0:HALO, :] = jnp.zeros((HALO, D_POOL), f32)

    x = x_ref[...]
    ms = jnp.mean(x * x, axis=-1, keepdims=True)
    h = (x * lax.rsqrt(ms + EPS) * g1_ref[...]).astype(bf16)
    proj_m = jnp.dot(h, w_ref[:, :N_MIX_IN], preferred_element_type=f32)
    proj_a = jnp.dot(h, w_ref[:, N_MIX_IN:], preferred_element_type=f32)
    cv = proj_m[:, 0:2 * D_CONV]
    pin = proj_m[:, 2 * D_CONV:2 * D_CONV + D_POOL]
    fg = proj_m[:, 2 * D_CONV + D_POOL:N_MIX_IN]
    q = proj_a[:, 0:D_ATT]
    k = proj_a[:, D_ATT:2 * D_ATT]
    v = proj_a[:, 2 * D_ATT:3 * D_ATT]

    grp = grp_ref[...]

    def head_norm(t, g_ref):
        tt = (t * t).astype(bf16)
        halves = [jnp.dot(tt[:, c * 256:(c + 1) * 256], grp, preferred_element_type=f32)
                  for c in range(D_ATT // 256)]
        msq = jnp.concatenate(halves, axis=1)
        return t * lax.rsqrt(msq + EPS) * g_ref[...]

    qn = head_norm(q, gq_ref)
    kn = head_norm(k, gk_ref).astype(bf16)

    lane = lax.broadcasted_iota(jnp.int32, (tm, FG_LANES), 1)
    lf = jax.nn.log_sigmoid(fg + bf_ref[...]) * LOG2E
    l0, l1, l2 = _split3(lf)
    packed = _pack_groups(lane, l0, l1, l2, 0.0).astype(bf16)
    part = jnp.dot(tri_ref[...], packed, preferred_element_type=f32)
    g = ATT_HEADS
    tot = (part + pltpu.roll(part, g, 1) + pltpu.roll(part, 2 * g, 1)
           + pltpu.roll(part, FG_LANES - g, 1) + pltpu.roll(part, FG_LANES - 2 * g, 1))
    cum = tot + carry_ref[0:1, :]
    carry_ref[...] = jnp.broadcast_to(cum[tm - 1:tm, :], carry_ref.shape)

    c0, c1, c2 = _split3(cum)
    ones_lane = jnp.where(lane == _SPLITS * ATT_HEADS, 1.0, 0.0)
    aug_f32 = _pack_groups(lane, c0, c1, c2, ones_lane)
    aug_src = aug_f32.astype(bf16)
    ka = jnp.dot(aug_src, ek_ref[...], preferred_element_type=f32).astype(bf16)
    qat = jnp.dot(eqt_ref[...], aug_f32.T.astype(bf16),
                  preferred_element_type=f32).astype(bf16)

    pieces = []
    for p in range(N_PAIRS):
        sl = slice(p * LANES, (p + 1) * LANES)
        pieces += [kn[:, sl], ka[:, sl]]
    kk_ref[...] = jnp.concatenate(pieces, axis=1)

    qnt = qn.T.astype(bf16)
    vt = v.T.astype(bf16)
    for p in range(N_PAIRS):
        sl = slice(p * LANES, (p + 1) * LANES)
        qt_ref[0, p, 0] = jnp.concatenate([qnt[sl, :], qat[sl, :]], axis=0)
        vt_ref[0, p, 0] = vt[sl, :]

    conv_ref[...] = _conformer_conv(cv, hbuf, dww_ref, dwb_ref, lng_ref, lnb_ref,
                                    pw_ref).astype(bf16)
    pool_ref[...] = _multiscale_pool(pin, ubuf, poolw_ref, pscale_ref, t_in_seq).astype(bf16)


def _in_proj_mix(x2, g1, w_cat, bf_pad, gq, gk, consts, mix_params, batch, seq):
    t = x2.shape[0]
    tm = ROW_TILE
    tiles_per_seq = seq // tm
    n_tiles = t // tm
    const = lambda shape: pl.BlockSpec(shape, lambda i: (0,) * len(shape))
    return pl.pallas_call(
        functools.partial(_in_kernel, tiles_per_seq=tiles_per_seq),
        grid=(n_tiles,),
        in_specs=[
            pl.BlockSpec((tm, D_MODEL), lambda i: (i, 0)),
            const((1, D_MODEL)),
            const((D_MODEL, N_PROJ)),
            const((1, FG_LANES)),
            const((1, D_ATT)),
            const((1, D_ATT)),
            const((256, 256)),
            const((tm, tm)),
            const((N_PAIRS * LANES, LANES)),
            const((LANES, N_PAIRS * LANES)),
            const((HALO, D_CONV)),
            const((1, D_CONV)),
            const((1, D_CONV)),
            const((1, D_CONV)),
            const((D_CONV, D_CONV)),
            const((D_POOL, D_POOL)),
            const((1, D_POOL)),
        ],
        out_specs=[
            pl.BlockSpec((1, N_PAIRS, 1, 2 * LANES, tm),
                         lambda i: (i // tiles_per_seq, 0, i % tiles_per_seq, 0, 0)),
            pl.BlockSpec((tm, 2 * D_ATT), lambda i: (i, 0)),
            pl.BlockSpec((1, N_PAIRS, 1, LANES, tm),
                         lambda i: (i // tiles_per_seq, 0, i % tiles_per_seq, 0, 0)),
            pl.BlockSpec((tm, D_CONV), lambda i: (i, 0)),
            pl.BlockSpec((tm, D_POOL), lambda i: (i, 0)),
        ],
        out_shape=[
            jax.ShapeDtypeStruct((batch, N_PAIRS, tiles_per_seq, 2 * LANES, tm), bf16),
            jax.ShapeDtypeStruct((t, 2 * D_ATT), bf16),
            jax.ShapeDtypeStruct((batch, N_PAIRS, tiles_per_seq, LANES, tm), bf16),
            jax.ShapeDtypeStruct((t, D_CONV), bf16),
            jax.ShapeDtypeStruct((t, D_POOL), bf16),
        ],
        scratch_shapes=[pltpu.VMEM((SUBLANES, FG_LANES), f32),
                        pltpu.VMEM((HALO + tm, D_CONV), f32),
                        pltpu.VMEM((HALO + tm, D_POOL), f32)],
        compiler_params=pltpu.CompilerParams(
            dimension_semantics=("arbitrary",), vmem_limit_bytes=VMEM_LIMIT),
        name="in_proj_mix",
    )(x2, g1, w_cat, bf_pad, gq, gk, *consts, *mix_params)


def _block_schedule(nq):
    blocks = [(i, j) for i in range(nq) for j in range(i)]
    n_unmasked = len(blocks)
    blocks += [(i, i) for i in range(nq)]
    bi = np.asarray([b[0] for b in blocks], np.int32)
    bj = np.asarray([b[1] for b in blocks], np.int32)
    return bi, bj, n_unmasked


def _att_kernel(bi_ref, bj_ref, qt_ref, kk_ref, vt_ref, o_ref, qh_ref, s0_ref, s1_ref,
                mt0_ref, mt1_ref, m_ref, acc_ref, *, n_unmasked):
    nq, _, _, tq = qh_ref.shape
    n_blocks = n_unmasked + nq
    feat = lax.broadcasted_iota(jnp.int32, (2 * LANES, 1), 0)
    aug = feat - LANES
    row = lax.broadcasted_iota(jnp.int32, (LANES, 1), 0)

    def in_range(v, lo, hi):
        return (v >= lo) & (v < hi)

    s = _SPLITS
    keep = [
        in_range(feat, 0, HEAD_DIM) | in_range(aug, 0, s) | in_range(aug, 2 * s, 3 * s),
        in_range(feat, HEAD_DIM, LANES) | in_range(aug, s, 2 * s) | in_range(aug, 3 * s, 4 * s),
    ]
    for it in range(nq):
        qt = qt_ref[0, 0, it]
        for hh in range(2):
            qh_ref[it, hh] = jnp.where(keep[hh], qt, jnp.zeros_like(qt))
    vkeep = [row < HEAD_DIM, row >= HEAD_DIM]
    m_ref[...] = jnp.full(m_ref.shape, NEG, f32)
    acc_ref[...] = jnp.zeros(acc_ref.shape, f32)
    s_refs = (s0_ref, s1_ref)
    mt_refs = (mt0_ref, mt1_ref)

    hq = tq // 2

    def produce(slot, n, hh, diag):
        i = bi_ref[n]
        off = pl.multiple_of(bj_ref[n] * tq, tq)
        if not diag:
            kt = kk_ref[0, pl.ds(off, tq), :]
            st = jnp.dot(kt, qh_ref[i, hh], preferred_element_type=f32)
            s_refs[slot][hh] = st
            mt_refs[slot][hh] = jnp.max(st, axis=0, keepdims=True)
        else:
            k_early = kk_ref[0, pl.ds(off, hq), :]
            k_late = kk_ref[0, pl.ds(pl.multiple_of(off + hq, hq), hq), :]
            s_refs[slot][hh, 0:hq, :] = jnp.dot(k_early, qh_ref[i, hh],
                                                preferred_element_type=f32)
            s_refs[slot][hh, hq:tq, hq:tq] = jnp.dot(k_late, qh_ref[i, hh, :, hq:tq],
                                                     preferred_element_type=f32)

    def update(i, hh, cols, st, mt, vh):
        m = m_ref[i, hh, :, cols]
        m_new = jnp.maximum(m, mt)
        p = jnp.exp2(st - m_new).astype(bf16)
        alpha = jnp.exp2(m - m_new)
        acc_ref[i, hh, :, cols] = (acc_ref[i, hh, :, cols] * alpha
                                   + jnp.dot(vh, p, preferred_element_type=f32))
        m_ref[i, hh, :, cols] = m_new

    def consume(slot, n, hh, diag):
        i = bi_ref[n]
        vt = vt_ref[0, 0, bj_ref[n]]
        vh = jnp.where(vkeep[hh], vt, jnp.ones_like(vt))
        if not diag:
            update(i, hh, slice(0, tq), s_refs[slot][hh], mt_refs[slot][hh], vh)
        else:
            for cols, nk in ((slice(0, hq), hq), (slice(hq, tq), tq)):
                st = s_refs[slot][hh, 0:nk, cols]
                kidx = lax.broadcasted_iota(jnp.int32, st.shape, 0)
                qidx = lax.broadcasted_iota(jnp.int32, st.shape, 1) + cols.start
                st = jnp.where(kidx <= qidx, st, NEG)
                update(i, hh, cols, st, jnp.max(st, axis=0, keepdims=True), vh[:, 0:nk])

    def step(n, slot, last=False):
        static = isinstance(n, int)
        diag = static and n >= n_unmasked
        next_diag = static and n + 1 >= n_unmasked
        for hh in range(2):
            if not last:
                produce(1 - slot, n + 1, hh, next_diag)
            consume(slot, n, hh, diag)

    for hh in range(2):
        produce(0, 0, hh, n_unmasked == 0)
    looped = (max(n_unmasked - 1, 0) // ATT_UNROLL) * ATT_UNROLL
    if looped:
        def body(t, c):
            for u in range(ATT_UNROLL):
                step(t * ATT_UNROLL + u, u % 2)
            return c
        lax.fori_loop(0, looped // ATT_UNROLL, body, 0)
    for n in range(looped, n_blocks):
        step(n, n % 2, last=(n == n_blocks - 1))

    for it in range(nq):
        acc0 = acc_ref[it, 0]
        acc1 = acc_ref[it, 1]
        out0 = acc0 / acc0[HEAD_DIM:HEAD_DIM + 1, :]
        out1 = acc1 / acc1[0:1, :]
        ot = jnp.where(row < HEAD_DIM, out0, out1)
        o_ref[0, it * tq:(it + 1) * tq, :] = ot.T.astype(bf16)


def _attention(qq, kk, vt, batch, seq):
    tq = ROW_TILE
    nq = seq // tq
    bi, bj, n_unmasked = _block_schedule(nq)
    grid_spec = pltpu.PrefetchScalarGridSpec(
        num_scalar_prefetch=2,
        grid=(batch, N_PAIRS),
        in_specs=[
            pl.BlockSpec((1, 1, nq, 2 * LANES, tq), lambda b, p, bi, bj: (b, p, 0, 0, 0)),
            pl.BlockSpec((1, seq, 2 * LANES), lambda b, p, bi, bj: (b, 0, p)),
            pl.BlockSpec((1, 1, nq, LANES, tq), lambda b, p, bi, bj: (b, p, 0, 0, 0)),
        ],
        out_specs=pl.BlockSpec((1, seq, LANES), lambda b, p, bi, bj: (b, 0, p)),
        scratch_shapes=[pltpu.VMEM((nq, 2, 2 * LANES, tq), bf16),
                        pltpu.VMEM((2, tq, tq), f32), pltpu.VMEM((2, tq, tq), f32),
                        pltpu.VMEM((2, 1, tq), f32), pltpu.VMEM((2, 1, tq), f32),
                        pltpu.VMEM((nq, 2, 1, tq), f32), pltpu.VMEM((nq, 2, LANES, tq), f32)],
    )
    return pl.pallas_call(
        functools.partial(_att_kernel, n_unmasked=n_unmasked),
        grid_spec=grid_spec,
        out_shape=jax.ShapeDtypeStruct((batch, seq, D_ATT), bf16),
        compiler_params=pltpu.CompilerParams(
            dimension_semantics=("arbitrary", "arbitrary"), vmem_limit_bytes=VMEM_LIMIT),
        name="fox_attention",
    )(jnp.asarray(bi), jnp.asarray(bj), qq, kk, vt)


def _ffn_kernel(x_ref, att_ref, conv_ref, pool_ref, wo_ref, g2_ref, wup_ref, dw_ref, wdn_ref,
                out_ref, tail_ref, act_ref, *, tiles_per_seq):
    tm = x_ref.shape[0]

    @pl.when(pl.program_id(0) % tiles_per_seq == 0)
    def _():
        tail_ref[...] = jnp.zeros_like(tail_ref)

    mix = jnp.concatenate([att_ref[...], conv_ref[...], pool_ref[...]], axis=1)
    x1 = x_ref[...] + jnp.dot(mix, wo_ref[...], preferred_element_type=f32)
    ms = jnp.mean(x1 * x1, axis=-1, keepdims=True)
    h2 = (x1 * lax.rsqrt(ms + EPS) * g2_ref[...]).astype(bf16)

    def up_conv(col):
        cols = slice(col, col + FF_CHUNK)
        up = jnp.dot(h2, wup_ref[:, cols], preferred_element_type=f32)
        ext = jnp.concatenate([tail_ref[:, cols], up], axis=0)
        tail_ref[:, cols] = up[tm - SUBLANES:tm, :]
        s1 = pltpu.roll(ext, 1, 0)[SUBLANES:, :]
        s2 = pltpu.roll(ext, 2, 0)[SUBLANES:, :]
        w = dw_ref[:, cols]
        return w[0:1, :] * s2 + w[1:2, :] * s1 + w[2:3, :] * up

    for c in range(N_FF_CHUNKS):
        gate = up_conv(c * FF_CHUNK)
        val = up_conv(D_FF + c * FF_CHUNK)
        act_ref[:, c * FF_CHUNK:(c + 1) * FF_CHUNK] = (
            gate * jax.nn.sigmoid(gate) * val).astype(bf16)

    out_ref[...] = x1 + jnp.dot(act_ref[...], wdn_ref[...], preferred_element_type=f32)


def _ffn(x2, att, conv, pool, wo, g2, wup, dw, wdn, layer, seq):
    t = x2.shape[0]
    tm = ROW_TILE
    tiles_per_seq = seq // tm
    resident = lambda shape: pl.BlockSpec(shape, lambda i: (0,) * len(shape),
                                          pipeline_mode=pl.Buffered(1))
    of_layer = lambda shape: pl.BlockSpec((None,) + shape, lambda i: (layer, 0, 0),
                                          pipeline_mode=pl.Buffered(1))
    return pl.pallas_call(
        functools.partial(_ffn_kernel, tiles_per_seq=tiles_per_seq),
        grid=(t // tm,),
        in_specs=[
            pl.BlockSpec((tm, D_MODEL), lambda i: (i, 0)),
            pl.BlockSpec((tm, D_ATT), lambda i: (i, 0)),
            pl.BlockSpec((tm, D_CONV), lambda i: (i, 0)),
            pl.BlockSpec((tm, D_POOL), lambda i: (i, 0)),
            of_layer((D_MODEL, D_MODEL)),
            resident((1, D_MODEL)),
            of_layer((D_MODEL, 2 * D_FF)),
            resident((SUBLANES, 2 * D_FF)),
            of_layer((D_FF, D_MODEL)),
        ],
        out_specs=pl.BlockSpec((tm, D_MODEL), lambda i: (i, 0)),
        out_shape=jax.ShapeDtypeStruct((t, D_MODEL), f32),
        scratch_shapes=[pltpu.VMEM((SUBLANES, 2 * D_FF), f32),
                        pltpu.VMEM((tm, D_FF), bf16)],
        compiler_params=pltpu.CompilerParams(
            dimension_semantics=("arbitrary",), vmem_limit_bytes=VMEM_LIMIT),
        name="out_proj_ffn",
    )(x2, att, conv, pool, wo, g2, wup, dw, wdn)


def _constants():
    grp = np.kron(np.eye(256 // HEAD_DIM, dtype=np.float32),
                  np.full((HEAD_DIM, HEAD_DIM), 1.0 / HEAD_DIM, np.float32))
    tri = np.tril(np.ones((ROW_TILE, ROW_TILE), np.float32))
    eq, ek = _placement_matrices()
    return tuple(jnp.asarray(a, bf16) for a in (grp, tri, eq.T, ek))


def kernel(x, norm1_g, w_in, b_f, q_norm_g, k_norm_g, conv_dw_w, conv_dw_b, conv_ln_g, conv_ln_b,
           conv_pw_w, pool_w, pool_scale, w_out, norm2_g, w_up, ffn_dw_w, w_down):
    batch, seq, _ = x.shape
    depth = w_in.shape[0]
    assert seq % ROW_TILE == 0
    consts = _constants()
    x2 = x.reshape(batch * seq, D_MODEL)
    w_out_b, w_up_b, w_down_b = (w.astype(bf16) for w in (w_out, w_up, w_down))

    for l in range(depth):
        wi = w_in[l]
        o = 3 * D_ATT
        w_fg = wi[:, o:o + ATT_HEADS]
        o += ATT_HEADS
        w_cv = wi[:, o:o + 2 * D_CONV]
        o += 2 * D_CONV
        w_pl = wi[:, o:o + D_POOL]
        fg_pad = jnp.zeros((D_MODEL, FG_LANES - _SPLITS * ATT_HEADS), f32)
        w_cat = jnp.concatenate([w_cv, w_pl] + [w_fg] * _SPLITS + [fg_pad, wi[:, :3 * D_ATT]],
                                axis=1).astype(bf16)
        bf_pad = jnp.concatenate([b_f[l]] * _SPLITS
                                 + [jnp.zeros((FG_LANES - _SPLITS * ATT_HEADS,), f32)])[None, :]
        gq = jnp.tile(q_norm_g[l] * (ATT_SCALE * LOG2E), ATT_HEADS)[None, :]
        gk = jnp.tile(k_norm_g[l], ATT_HEADS)[None, :]
        dww = jnp.concatenate([conv_dw_w[l], jnp.zeros((HALO - CONV_WIDTH, D_CONV), f32)], axis=0)
        poolw = jax.scipy.linalg.block_diag(*[pool_w[l, g] for g in range(len(POOL_WINDOWS))])
        mix_params = (dww, conv_dw_b[l][None, :], conv_ln_g[l][None, :], conv_ln_b[l][None, :],
                      conv_pw_w[l].astype(bf16), poolw.astype(bf16), pool_scale[l][None, :])

        qt, kk, vt, conv, pool = _in_proj_mix(x2, norm1_g[l][None, :], w_cat, bf_pad, gq, gk,
                                              consts, mix_params, batch, seq)

        att = _attention(qt, kk.reshape(batch, seq, 2 * D_ATT),
                         vt, batch, seq).reshape(batch * seq, D_ATT)

        dw = jnp.concatenate(
            [ffn_dw_w[l], jnp.zeros((SUBLANES - FFN_CONV_WIDTH, 2 * D_FF), f32)], axis=0)
        x2 = _ffn(x2, att, conv, pool, w_out_b, norm2_g[l][None, :], w_up_b, dw, w_down_b, l, seq)

    return x2.reshape(batch, seq, D_MODEL)
```

```python
import functools

import numpy as np
import jax
import jax.numpy as jnp
from jax import lax
from jax.experimental import pallas as pl
from jax.experimental.pallas import tpu as pltpu

D_MODEL = 1024
ATT_HEADS = 8
HEAD_DIM = 64
D_ATT = ATT_HEADS * HEAD_DIM
N_PAIRS = ATT_HEADS // 2
ATT_SCALE = HEAD_DIM ** -0.5
D_CONV = 256
CONV_WIDTH = 31
POOL_WINDOWS = (2, 4, 8, 16)
POOL_GROUP = 64
D_POOL = len(POOL_WINDOWS) * POOL_GROUP
D_FF = 2816
FFN_CONV_WIDTH = 3
EPS = 1e-6
NEG = -1e30
LOG2E = 1.4426950408889634

LANES = 128
SUBLANES = 8
FG_LANES = LANES
N_MIX_IN = 2 * D_CONV + D_POOL + FG_LANES
N_PROJ = N_MIX_IN + 3 * D_ATT
ROW_TILE = 512
FFN_TILE = 1024
HALO = 32
POOL_HALO = 16
FF_CHUNK = 256
N_FF_CHUNKS = D_FF // FF_CHUNK
ATT_UNROLL = 12
VMEM_LIMIT = 56 * 1024 * 1024

f32 = jnp.float32
bf16 = jnp.bfloat16

_SPLITS = 3


def _placement_matrices():
    eq = np.zeros((LANES, N_PAIRS * LANES), np.float32)
    ek = np.zeros((LANES, N_PAIRS * LANES), np.float32)
    for h in range(ATT_HEADS):
        p, hh = divmod(h, 2)
        for s in range(_SPLITS):
            src = s * ATT_HEADS + h
            ek[src, p * LANES + hh * _SPLITS + s] = -1.0
            eq[src, p * LANES + 2 * _SPLITS + hh * _SPLITS + s] = 1.0
    one_lane = _SPLITS * ATT_HEADS
    for p in range(N_PAIRS):
        ek[one_lane, p * LANES + 2 * _SPLITS:p * LANES + 4 * _SPLITS] = 1.0
        eq[one_lane, p * LANES:p * LANES + 2 * _SPLITS] = 1.0
    return eq, ek


def _split3(c):
    hi = c.astype(bf16).astype(f32)
    r1 = c - hi
    mid = r1.astype(bf16).astype(f32)
    lo = r1 - mid
    return hi, mid, lo


def _pack_groups(lane, a0, a1, a2, rest):
    return jnp.where(lane < ATT_HEADS, a0,
                     jnp.where(lane < 2 * ATT_HEADS, a1,
                               jnp.where(lane < 3 * ATT_HEADS, a2, rest)))


def _conformer_conv(cv, hbuf, dww_ref, dwb_ref, lng_ref, lnb_ref, pw_ref):
    tm = cv.shape[0]
    hbuf[HALO:HALO + tm, :] = cv[:, :D_CONV] * jax.nn.sigmoid(cv[:, D_CONV:])
    ext = tm + SUBLANES
    acc = None
    for r in range(SUBLANES):
        part = None
        for a in range((CONV_WIDTH - 1 - r) // SUBLANES + 1):
            tap = CONV_WIDTH - 1 - (SUBLANES * a + r)
            lo = HALO - SUBLANES - SUBLANES * a
            term = dww_ref[tap:tap + 1, :] * hbuf[lo:lo + ext, :]
            part = term if part is None else part + term
        if r:
            part = pltpu.roll(part, r, 0)
        part = part[SUBLANES:, :]
        acc = part if acc is None else acc + part
    acc = acc + dwb_ref[...]
    hbuf[0:HALO, :] = hbuf[tm:tm + HALO, :]
    mu = jnp.mean(acc, axis=-1, keepdims=True)
    cen = acc - mu
    var = jnp.mean(cen * cen, axis=-1, keepdims=True)
    y = cen * lax.rsqrt(var + EPS) * lng_ref[...] + lnb_ref[...]
    y = y * jax.nn.sigmoid(y)
    return jnp.dot(y.astype(bf16), pw_ref[...], preferred_element_type=f32)


def _multiscale_pool(u, ubuf, poolw_ref, pscale_ref, t_in_seq):
    tm = u.shape[0]
    ubuf[HALO:HALO + tm, :] = u
    e = ubuf[HALO - POOL_HALO:HALO + tm, :]
    sums = {}
    w = 1
    while w < max(POOL_WINDOWS):
        e = e + pltpu.roll(e, w, 0)
        w *= 2
        sums[w] = e[POOL_HALO:, :]
    ubuf[0:HALO, :] = ubuf[tm:tm + HALO, :]
    lane = lax.broadcasted_iota(jnp.int32, (tm, D_POOL), 1)
    row = lax.broadcasted_iota(jnp.int32, (tm, D_POOL), 0)
    pos = (row + t_in_seq * tm + 1).astype(f32)
    box = sums[POOL_WINDOWS[-1]]
    win = jnp.full((tm, D_POOL), float(POOL_WINDOWS[-1]), f32)
    for gi in range(len(POOL_WINDOWS) - 2, -1, -1):
        sel = lane < (gi + 1) * POOL_GROUP
        box = jnp.where(sel, sums[POOL_WINDOWS[gi]], box)
        win = jnp.where(sel, float(POOL_WINDOWS[gi]), win)
    d = box / jnp.minimum(pos, win) - u
    yp = jnp.dot(d.astype(bf16), poolw_ref[...], preferred_element_type=f32)
    return yp * pscale_ref[...]


def _in_kernel(x_ref, g1_ref, w_ref, bf_ref, gq_ref, gk_ref, grp_ref, tri_ref, eqt_ref, ek_ref,
               dww_ref, dwb_ref, lng_ref, lnb_ref, pw_ref, poolw_ref, pscale_ref,
               qt_ref, kk_ref, vt_ref, conv_ref, pool_ref, carry_ref, hbuf, ubuf, *, tiles_per_seq):
    tm = x_ref.shape[0]
    t_in_seq = pl.program_id(0) % tiles_per_seq

    @pl.when(t_in_seq == 0)
    def _():
        carry_ref[...] = jnp.zeros_like(carry_ref)
        hbuf[0:HALO, :] = jnp.zeros((HALO, D_CONV), f32)
        ubuf[0:HALO, :] = jnp.zeros((HALO, D_POOL), f32)

    x = x_ref[...]
    ms = jnp.mean(x * x, axis=-1, keepdims=True)
    h = (x * lax.rsqrt(ms + EPS) * g1_ref[...]).astype(bf16)
    proj_m = jnp.dot(h, w_ref[:, :N_MIX_IN], preferred_element_type=f32)
    proj_a = jnp.dot(h, w_ref[:, N_MIX_IN:], preferred_element_type=f32)
    cv = proj_m[:, 0:2 * D_CONV]
    pin = proj_m[:, 2 * D_CONV:2 * D_CONV + D_POOL]
    fg = proj_m[:, 2 * D_CONV + D_POOL:N_MIX_IN]
    q = proj_a[:, 0:D_ATT]
    k = proj_a[:, D_ATT:2 * D_ATT]
    v = proj_a[:, 2 * D_ATT:3 * D_ATT]

    grp = grp_ref[...]

    def head_norm(t, g_ref):
        tt = (t * t).astype(bf16)
        halves = [jnp.dot(tt[:, c * 256:(c + 1) * 256], grp, preferred_element_type=f32)
                  for c in range(D_ATT // 256)]
        msq = jnp.concatenate(halves, axis=1)
        return t * lax.rsqrt(msq + EPS) * g_ref[...]

    qn = head_norm(q, gq_ref)
    kn = head_norm(k, gk_ref).astype(bf16)

    lane = lax.broadcasted_iota(jnp.int32, (tm, FG_LANES), 1)
    lf = jax.nn.log_sigmoid(fg + bf_ref[...]) * LOG2E
    l0, l1, l2 = _split3(lf)
    packed = _pack_groups(lane, l0, l1, l2, 0.0).astype(bf16)
    part = jnp.dot(tri_ref[...], packed, preferred_element_type=f32)
    g = ATT_HEADS
    tot = (part + pltpu.roll(part, g, 1) + pltpu.roll(part, 2 * g, 1)
           + pltpu.roll(part, FG_LANES - g, 1) + pltpu.roll(part, FG_LANES - 2 * g, 1))
    cum = tot + carry_ref[0:1, :]
    carry_ref[...] = jnp.broadcast_to(cum[tm - 1:tm, :], carry_ref.shape)

    c0, c1, c2 = _split3(cum)
    ones_lane = jnp.where(lane == _SPLITS * ATT_HEADS, 1.0, 0.0)
    aug_f32 = _pack_groups(lane, c0, c1, c2, ones_lane)
    aug_src = aug_f32.astype(bf16)
    ka = jnp.dot(aug_src, ek_ref[...], preferred_element_type=f32).astype(bf16)
    qat = jnp.dot(eqt_ref[...], aug_f32.T.astype(bf16),
                  preferred_element_type=f32).astype(bf16)

    pieces = []
    for p in range(N_PAIRS):
        sl = slice(p * LANES, (p + 1) * LANES)
        pieces += [kn[:, sl], ka[:, sl]]
    kk_ref[...] = jnp.concatenate(pieces, axis=1)

    qnt = qn.T.astype(bf16)
    vt = v.T.astype(bf16)
    for p in range(N_PAIRS):
        sl = slice(p * LANES, (p + 1) * LANES)
        qt_ref[0, p, 0] = jnp.concatenate([qnt[sl, :], qat[sl, :]], axis=0)
        vt_ref[0, p, 0] = vt[sl, :]

    conv_ref[...] = _conformer_conv(cv, hbuf, dww_ref, dwb_ref, lng_ref, lnb_ref,
                                    pw_ref).astype(bf16)
    pool_ref[...] = _multiscale_pool(pin, ubuf, poolw_ref, pscale_ref, t_in_seq).astype(bf16)


def _in_proj_mix(x2, g1, w_cat, bf_pad, gq, gk, consts, mix_params, batch, seq):
    t = x2.shape[0]
    tm = ROW_TILE
    tiles_per_seq = seq // tm
    n_tiles = t // tm
    const = lambda shape: pl.BlockSpec(shape, lambda i: (0,) * len(shape))
    return pl.pallas_call(
        functools.partial(_in_kernel, tiles_per_seq=tiles_per_seq),
        grid=(n_tiles,),
        in_specs=[
            pl.BlockSpec((tm, D_MODEL), lambda i: (i, 0)),
            const((1, D_MODEL)),
            const((D_MODEL, N_PROJ)),
            const((1, FG_LANES)),
            const((1, D_ATT)),
            const((1, D_ATT)),
            const((256, 256)),
            const((tm, tm)),
            const((N_PAIRS * LANES, LANES)),
            const((LANES, N_PAIRS * LANES)),
            const((HALO, D_CONV)),
            const((1, D_CONV)),
            const((1, D_CONV)),
            const((1, D_CONV)),
            const((D_CONV, D_CONV)),
            const((D_POOL, D_POOL)),
            const((1, D_POOL)),
        ],
        out_specs=[
            pl.BlockSpec((1, N_PAIRS, 1, 2 * LANES, tm),
                         lambda i: (i // tiles_per_seq, 0, i % tiles_per_seq, 0, 0)),
            pl.BlockSpec((tm, 2 * D_ATT), lambda i: (i, 0)),
            pl.BlockSpec((1, N_PAIRS, 1, LANES, tm),
                         lambda i: (i // tiles_per_seq, 0, i % tiles_per_seq, 0, 0)),
            pl.BlockSpec((tm, D_CONV), lambda i: (i, 0)),
            pl.BlockSpec((tm, D_POOL), lambda i: (i, 0)),
        ],
        out_shape=[
            jax.ShapeDtypeStruct((batch, N_PAIRS, tiles_per_seq, 2 * LANES, tm), bf16),
            jax.ShapeDtypeStruct((t, 2 * D_ATT), bf16),
            jax.ShapeDtypeStruct((batch, N_PAIRS, tiles_per_seq, LANES, tm), bf16),
            jax.ShapeDtypeStruct((t, D_CONV), bf16),
            jax.ShapeDtypeStruct((t, D_POOL), bf16),
        ],
        scratch_shapes=[pltpu.VMEM((SUBLANES, FG_LANES), f32),
                        pltpu.VMEM((HALO + tm, D_CONV), f32),
                        pltpu.VMEM((HALO + tm, D_POOL), f32)],
        compiler_params=pltpu.CompilerParams(
            dimension_semantics=("arbitrary",), vmem_limit_bytes=VMEM_LIMIT),
        name="in_proj_mix",
    )(x2, g1, w_cat, bf_pad, gq, gk, *consts, *mix_params)


def _block_schedule(nq):
    blocks = [(i, j) for i in range(nq) for j in range(i)]
    n_unmasked = len(blocks)
    blocks += [(i, i) for i in range(nq)]
    bi = np.asarray([b[0] for b in blocks], np.int32)
    bj = np.asarray([b[1] for b in blocks], np.int32)
    return bi, bj, n_unmasked


def _att_kernel(bi_ref, bj_ref, qt_ref, kk_ref, vt_ref, o_ref, qh_ref, s0_ref, s1_ref,
                mt0_ref, mt1_ref, m_ref, acc_ref, *, n_unmasked):
    nq, _, _, tq = qh_ref.shape
    n_blocks = n_unmasked + nq
    feat = lax.broadcasted_iota(jnp.int32, (2 * LANES, 1), 0)
    aug = feat - LANES
    row = lax.broadcasted_iota(jnp.int32, (LANES, 1), 0)

    def in_range(v, lo, hi):
        return (v >= lo) & (v < hi)

    s = _SPLITS
    keep = [
        in_range(feat, 0, HEAD_DIM) | in_range(aug, 0, s) | in_range(aug, 2 * s, 3 * s),
        in_range(feat, HEAD_DIM, LANES) | in_range(aug, s, 2 * s) | in_range(aug, 3 * s, 4 * s),
    ]
    for it in range(nq):
        qt = qt_ref[0, 0, it]
        for hh in range(2):
            qh_ref[it, hh] = jnp.where(keep[hh], qt, jnp.zeros_like(qt))
    vkeep = [row < HEAD_DIM, row >= HEAD_DIM]
    m_ref[...] = jnp.full(m_ref.shape, NEG, f32)
    acc_ref[...] = jnp.zeros(acc_ref.shape, f32)
    s_refs = (s0_ref, s1_ref)
    mt_refs = (mt0_ref, mt1_ref)

    hq = tq // 2

    def produce(slot, n, hh, diag):
        i = bi_ref[n]
        off = pl.multiple_of(bj_ref[n] * tq, tq)
        if not diag:
            kt = kk_ref[0, pl.ds(off, tq), :]
            st = jnp.dot(kt, qh_ref[i, hh], preferred_element_type=f32)
            s_refs[slot][hh] = st
            mt_refs[slot][hh] = jnp.max(st, axis=0, keepdims=True)
        else:
            k_early = kk_ref[0, pl.ds(off, hq), :]
            k_late = kk_ref[0, pl.ds(pl.multiple_of(off + hq, hq), hq), :]
            s_refs[slot][hh, 0:hq, :] = jnp.dot(k_early, qh_ref[i, hh],
                                                preferred_element_type=f32)
            s_refs[slot][hh, hq:tq, hq:tq] = jnp.dot(k_late, qh_ref[i, hh, :, hq:tq],
                                                     preferred_element_type=f32)

    def update(i, hh, cols, st, mt, vh):
        m = m_ref[i, hh, :, cols]
        m_new = jnp.maximum(m, mt)
        p = jnp.exp2(st - m_new).astype(bf16)
        alpha = jnp.exp2(m - m_new)
        acc_ref[i, hh, :, cols] = (acc_ref[i, hh, :, cols] * alpha
                                   + jnp.dot(vh, p, preferred_element_type=f32))
        m_ref[i, hh, :, cols] = m_new

    def consume(slot, n, hh, diag):
        i = bi_ref[n]
        vt = vt_ref[0, 0, bj_ref[n]]
        vh = jnp.where(vkeep[hh], vt, jnp.ones_like(vt))
        if not diag:
            update(i, hh, slice(0, tq), s_refs[slot][hh], mt_refs[slot][hh], vh)
        else:
            for cols, nk in ((slice(0, hq), hq), (slice(hq, tq), tq)):
                st = s_refs[slot][hh, 0:nk, cols]
                kidx = lax.broadcasted_iota(jnp.int32, st.shape, 0)
                qidx = lax.broadcasted_iota(jnp.int32, st.shape, 1) + cols.start
                st = jnp.where(kidx <= qidx, st, NEG)
                update(i, hh, cols, st, jnp.max(st, axis=0, keepdims=True), vh[:, 0:nk])

    def step(n, slot, last=False):
        static = isinstance(n, int)
        diag = static and n >= n_unmasked
        next_diag = static and n + 1 >= n_unmasked
        for hh in range(2):
            if not last:
                produce(1 - slot, n + 1, hh, next_diag)
            consume(slot, n, hh, diag)

    for hh in range(2):
        produce(0, 0, hh, n_unmasked == 0)
    looped = (max(n_unmasked - 1, 0) // ATT_UNROLL) * ATT_UNROLL
    if looped:
        def body(t, c):
            for u in range(ATT_UNROLL):
                step(t * ATT_UNROLL + u, u % 2)
            return c
        lax.fori_loop(0, looped // ATT_UNROLL, body, 0)
    for n in range(looped, n_blocks):
        step(n, n % 2, last=(n == n_blocks - 1))

    for it in range(nq):
        acc0 = acc_ref[it, 0]
        acc1 = acc_ref[it, 1]
        out0 = acc0 / acc0[HEAD_DIM:HEAD_DIM + 1, :]
        out1 = acc1 / acc1[0:1, :]
        ot = jnp.where(row < HEAD_DIM, out0, out1)
        o_ref[0, it * tq:(it + 1) * tq, :] = ot.T.astype(bf16)


def _attention(qq, kk, vt, batch, seq):
    tq = ROW_TILE
    nq = seq // tq
    bi, bj, n_unmasked = _block_schedule(nq)
    grid_spec = pltpu.PrefetchScalarGridSpec(
        num_scalar_prefetch=2,
        grid=(batch, N_PAIRS),
        in_specs=[
            pl.BlockSpec((1, 1, nq, 2 * LANES, tq), lambda b, p, bi, bj: (b, p, 0, 0, 0)),
            pl.BlockSpec((1, seq, 2 * LANES), lambda b, p, bi, bj: (b, 0, p)),
            pl.BlockSpec((1, 1, nq, LANES, tq), lambda b, p, bi, bj: (b, p, 0, 0, 0)),
        ],
        out_specs=pl.BlockSpec((1, seq, LANES), lambda b, p, bi, bj: (b, 0, p)),
        scratch_shapes=[pltpu.VMEM((nq, 2, 2 * LANES, tq), bf16),
                        pltpu.VMEM((2, tq, tq), f32), pltpu.VMEM((2, tq, tq), f32),
                        pltpu.VMEM((2, 1, tq), f32), pltpu.VMEM((2, 1, tq), f32),
                        pltpu.VMEM((nq, 2, 1, tq), f32), pltpu.VMEM((nq, 2, LANES, tq), f32)],
    )
    return pl.pallas_call(
        functools.partial(_att_kernel, n_unmasked=n_unmasked),
        grid_spec=grid_spec,
        out_shape=jax.ShapeDtypeStruct((batch, seq, D_ATT), bf16),
        compiler_params=pltpu.CompilerParams(
            dimension_semantics=("arbitrary", "arbitrary"), vmem_limit_bytes=VMEM_LIMIT),
        name="fox_attention",
    )(jnp.asarray(bi), jnp.asarray(bj), qq, kk, vt)


def _ffn_kernel(x_ref, att_ref, conv_ref, pool_ref, wo_ref, g2_ref, wup_ref, dw_ref, wdn_ref,
                out_ref, tail_ref, act_ref, *, tiles_per_seq):
    tm = x_ref.shape[0]

    @pl.when(pl.program_id(0) % tiles_per_seq == 0)
    def _():
        tail_ref[...] = jnp.zeros_like(tail_ref)

    mix = jnp.concatenate([att_ref[...], conv_ref[...], pool_ref[...]], axis=1)
    x1 = x_ref[...] + jnp.dot(mix, wo_ref[...], preferred_element_type=f32)
    ms = jnp.mean(x1 * x1, axis=-1, keepdims=True)
    h2 = (x1 * lax.rsqrt(ms + EPS) * g2_ref[...]).astype(bf16)

    def up_conv(col):
        cols = slice(col, col + FF_CHUNK)
        up = jnp.dot(h2, wup_ref[:, cols], preferred_element_type=f32)
        ext = jnp.concatenate([tail_ref[:, cols], up], axis=0)
        tail_ref[:, cols] = up[tm - SUBLANES:tm, :]
        s1 = pltpu.roll(ext, 1, 0)[SUBLANES:, :]
        s2 = pltpu.roll(ext, 2, 0)[SUBLANES:, :]
        w = dw_ref[:, cols]
        return w[0:1, :] * s2 + w[1:2, :] * s1 + w[2:3, :] * up

    for c in range(N_FF_CHUNKS):
        gate = up_conv(c * FF_CHUNK)
        val = up_conv(D_FF + c * FF_CHUNK)
        act_ref[:, c * FF_CHUNK:(c + 1) * FF_CHUNK] = (
            gate * jax.nn.sigmoid(gate) * val).astype(bf16)

    out_ref[...] = x1 + jnp.dot(act_ref[...], wdn_ref[...], preferred_element_type=f32)


def _ffn(x2, att, conv, pool, wo, g2, wup, dw, wdn, layer, seq):
    t = x2.shape[0]
    tm = FFN_TILE
    tiles_per_seq = seq // tm
    resident = lambda shape: pl.BlockSpec(shape, lambda i: (0,) * len(shape),
                                          pipeline_mode=pl.Buffered(1))
    of_layer = lambda shape: pl.BlockSpec((None,) + shape, lambda i: (layer, 0, 0),
                                          pipeline_mode=pl.Buffered(1))
    return pl.pallas_call(
        functools.partial(_ffn_kernel, tiles_per_seq=tiles_per_seq),
        grid=(t // tm,),
        in_specs=[
            pl.BlockSpec((tm, D_MODEL), lambda i: (i, 0)),
            pl.BlockSpec((tm, D_ATT), lambda i: (i, 0)),
            pl.BlockSpec((tm, D_CONV), lambda i: (i, 0)),
            pl.BlockSpec((tm, D_POOL), lambda i: (i, 0)),
            of_layer((D_MODEL, D_MODEL)),
            resident((1, D_MODEL)),
            of_layer((D_MODEL, 2 * D_FF)),
            resident((SUBLANES, 2 * D_FF)),
            of_layer((D_FF, D_MODEL)),
        ],
        out_specs=pl.BlockSpec((tm, D_MODEL), lambda i: (i, 0)),
        out_shape=jax.ShapeDtypeStruct((t, D_MODEL), f32),
        scratch_shapes=[pltpu.VMEM((SUBLANES, 2 * D_FF), f32),
                        pltpu.VMEM((tm, D_FF), bf16)],
        compiler_params=pltpu.CompilerParams(
            dimension_semantics=("arbitrary",), vmem_limit_bytes=VMEM_LIMIT),
        name="out_proj_ffn",
    )(x2, att, conv, pool, wo, g2, wup, dw, wdn)


def _constants():
    grp = np.kron(np.eye(256 // HEAD_DIM, dtype=np.float32),
                  np.full((HEAD_DIM, HEAD_DIM), 1.0 / HEAD_DIM, np.float32))
    tri = np.tril(np.ones((ROW_TILE, ROW_TILE), np.float32))
    eq, ek = _placement_matrices()
    return tuple(jnp.asarray(a, bf16) for a in (grp, tri, eq.T, ek))


def kernel(x, norm1_g, w_in, b_f, q_norm_g, k_norm_g, conv_dw_w, conv_dw_b, conv_ln_g, conv_ln_b,
           conv_pw_w, pool_w, pool_scale, w_out, norm2_g, w_up, ffn_dw_w, w_down):
    batch, seq, _ = x.shape
    depth = w_in.shape[0]
    assert seq % ROW_TILE == 0
    consts = _constants()
    x2 = x.reshape(batch * seq, D_MODEL)
    w_out_b, w_up_b, w_down_b = (w.astype(bf16) for w in (w_out, w_up, w_down))

    for l in range(depth):
        wi = w_in[l]
        o = 3 * D_ATT
        w_fg = wi[:, o:o + ATT_HEADS]
        o += ATT_HEADS
        w_cv = wi[:, o:o + 2 * D_CONV]
        o += 2 * D_CONV
        w_pl = wi[:, o:o + D_POOL]
        fg_pad = jnp.zeros((D_MODEL, FG_LANES - _SPLITS * ATT_HEADS), f32)
        w_cat = jnp.concatenate([w_cv, w_pl] + [w_fg] * _SPLITS + [fg_pad, wi[:, :3 * D_ATT]],
                                axis=1).astype(bf16)
        bf_pad = jnp.concatenate([b_f[l]] * _SPLITS
                                 + [jnp.zeros((FG_LANES - _SPLITS * ATT_HEADS,), f32)])[None, :]
        gq = jnp.tile(q_norm_g[l] * (ATT_SCALE * LOG2E), ATT_HEADS)[None, :]
        gk = jnp.tile(k_norm_g[l], ATT_HEADS)[None, :]
        dww = jnp.concatenate([conv_dw_w[l], jnp.zeros((HALO - CONV_WIDTH, D_CONV), f32)], axis=0)
        poolw = jax.scipy.linalg.block_diag(*[pool_w[l, g] for g in range(len(POOL_WINDOWS))])
        mix_params = (dww, conv_dw_b[l][None, :], conv_ln_g[l][None, :], conv_ln_b[l][None, :],
                      conv_pw_w[l].astype(bf16), poolw.astype(bf16), pool_scale[l][None, :])

        qt, kk, vt, conv, pool = _in_proj_mix(x2, norm1_g[l][None, :], w_cat, bf_pad, gq, gk,
                                              consts, mix_params, batch, seq)

        att = _attention(qt, kk.reshape(batch, seq, 2 * D_ATT),
                         vt, batch, seq).reshape(batch * seq, D_ATT)

        dw = jnp.concatenate(
            [ffn_dw_w[l], jnp.zeros((SUBLANES - FFN_CONV_WIDTH, 2 * D_FF), f32)], axis=0)
        x2 = _ffn(x2, att, conv, pool, w_out_b, norm2_g[l][None, :], w_up_b, dw, w_down_b, l, seq)

    return x2.reshape(batch, seq, D_MODEL)
```

```python
import functools

import numpy as np
import jax
import jax.numpy as jnp
from jax import lax
from jax.experimental import pallas as pl
from jax.experimental.pallas import tpu as pltpu

D_MODEL = 1024
ATT_HEADS = 8
HEAD_DIM = 64
D_ATT = ATT_HEADS * HEAD_DIM
N_PAIRS = ATT_HEADS // 2
ATT_SCALE = HEAD_DIM ** -0.5
D_CONV = 256
CONV_WIDTH = 31
POOL_WINDOWS = (2, 4, 8, 16)
POOL_GROUP = 64
D_POOL = len(POOL_WINDOWS) * POOL_GROUP
D_FF = 2816
FFN_CONV_WIDTH = 3
EPS = 1e-6
NEG = -1e30
LOG2E = 1.4426950408889634

LANES = 128
SUBLANES = 8
MXU_DIM = 256
FG_LANES = LANES
N_MIX_IN = 2 * D_CONV + D_POOL + FG_LANES
N_PROJ = N_MIX_IN + 3 * D_ATT
ROW_TILE = 512
FFN_TILE = 1024
HALO = 32
POOL_HALO = 16
FF_CHUNK = MXU_DIM
N_FF_CHUNKS = D_FF // FF_CHUNK
ATT_UNROLL = 12
VMEM_LIMIT = 56 * 1024 * 1024

f32 = jnp.float32
bf16 = jnp.bfloat16

_SPLITS = 3


def _placement_matrices():
    eq = np.zeros((LANES, N_PAIRS * LANES), np.float32)
    ek = np.zeros((LANES, N_PAIRS * LANES), np.float32)
    for h in range(ATT_HEADS):
        p, hh = divmod(h, 2)
        for s in range(_SPLITS):
            src = s * ATT_HEADS + h
            ek[src, p * LANES + hh * _SPLITS + s] = -1.0
            eq[src, p * LANES + 2 * _SPLITS + hh * _SPLITS + s] = 1.0
    one_lane = _SPLITS * ATT_HEADS
    for p in range(N_PAIRS):
        ek[one_lane, p * LANES + 2 * _SPLITS:p * LANES + 4 * _SPLITS] = 1.0
        eq[one_lane, p * LANES:p * LANES + 2 * _SPLITS] = 1.0
    return eq, ek


def _split3(c):
    hi = c.astype(bf16).astype(f32)
    r1 = c - hi
    mid = r1.astype(bf16).astype(f32)
    lo = r1 - mid
    return hi, mid, lo


def _pack_groups(lane, a0, a1, a2, rest):
    return jnp.where(lane < ATT_HEADS, a0,
                     jnp.where(lane < 2 * ATT_HEADS, a1,
                               jnp.where(lane < 3 * ATT_HEADS, a2, rest)))


def _conformer_conv(cv, hbuf, dww_ref, dwb_ref, lng_ref, lnb_ref, pw_ref):
    tm = cv.shape[0]
    hbuf[HALO:HALO + tm, :] = cv[:, :D_CONV] * jax.nn.sigmoid(cv[:, D_CONV:])
    ext = tm + SUBLANES
    acc = None
    for r in range(SUBLANES):
        part = None
        for a in range((CONV_WIDTH - 1 - r) // SUBLANES + 1):
            tap = CONV_WIDTH - 1 - (SUBLANES * a + r)
            lo = HALO - SUBLANES - SUBLANES * a
            term = dww_ref[tap:tap + 1, :] * hbuf[lo:lo + ext, :]
            part = term if part is None else part + term
        if r:
            part = pltpu.roll(part, r, 0)
        part = part[SUBLANES:, :]
        acc = part if acc is None else acc + part
    acc = acc + dwb_ref[...]
    hbuf[0:HALO, :] = hbuf[tm:tm + HALO, :]
    mu = jnp.mean(acc, axis=-1, keepdims=True)
    cen = acc - mu
    var = jnp.mean(cen * cen, axis=-1, keepdims=True)
    y = cen * lax.rsqrt(var + EPS) * lng_ref[...] + lnb_ref[...]
    y = y * jax.nn.sigmoid(y)
    return jnp.dot(y.astype(bf16), pw_ref[...], preferred_element_type=f32)


def _multiscale_pool(u, ubuf, poolw_ref, pscale_ref, t_in_seq):
    tm = u.shape[0]
    ubuf[HALO:HALO + tm, :] = u
    e = ubuf[HALO - POOL_HALO:HALO + tm, :]
    sums = {}
    w = 1
    while w < max(POOL_WINDOWS):
        e = e + pltpu.roll(e, w, 0)
        w *= 2
        sums[w] = e[POOL_HALO:, :]
    ubuf[0:HALO, :] = ubuf[tm:tm + HALO, :]
    lane = lax.broadcasted_iota(jnp.int32, (tm, D_POOL), 1)
    row = lax.broadcasted_iota(jnp.int32, (tm, D_POOL), 0)
    pos = (row + t_in_seq * tm + 1).astype(f32)
    box = sums[POOL_WINDOWS[-1]]
    win = jnp.full((tm, D_POOL), float(POOL_WINDOWS[-1]), f32)
    for gi in range(len(POOL_WINDOWS) - 2, -1, -1):
        sel = lane < (gi + 1) * POOL_GROUP
        box = jnp.where(sel, sums[POOL_WINDOWS[gi]], box)
        win = jnp.where(sel, float(POOL_WINDOWS[gi]), win)
    d = box / jnp.minimum(pos, win) - u
    yp = jnp.dot(d.astype(bf16), poolw_ref[...], preferred_element_type=f32)
    return yp * pscale_ref[...]


def _in_kernel(x_ref, g1_ref, w_ref, bf_ref, gq_ref, gk_ref, grp_ref, tri_ref, eqt_ref, ek_ref,
               dww_ref, dwb_ref, lng_ref, lnb_ref, pw_ref, poolw_ref, pscale_ref,
               qt_ref, kk_ref, vt_ref, conv_ref, pool_ref, carry_ref, hbuf, ubuf, *, tiles_per_seq):
    tm = x_ref.shape[0]
    t_in_seq = pl.program_id(0) % tiles_per_seq

    @pl.when(t_in_seq == 0)
    def _():
        carry_ref[...] = jnp.zeros_like(carry_ref)
        hbuf[0:HALO, :] = jnp.zeros((HALO, D_CONV), f32)
        ubuf[0:HALO, :] = jnp.zeros((HALO, D_POOL), f32)

    x = x_ref[...]
    ms = jnp.mean(x * x, axis=-1, keepdims=True)
    h = (x * lax.rsqrt(ms + EPS) * g1_ref[...]).astype(bf16)
    proj_m = jnp.dot(h, w_ref[:, :N_MIX_IN], preferred_element_type=f32)
    proj_a = jnp.dot(h, w_ref[:, N_MIX_IN:], preferred_element_type=f32)
    cv = proj_m[:, 0:2 * D_CONV]
    pin = proj_m[:, 2 * D_CONV:2 * D_CONV + D_POOL]
    fg = proj_m[:, 2 * D_CONV + D_POOL:N_MIX_IN]
    q = proj_a[:, 0:D_ATT]
    k = proj_a[:, D_ATT:2 * D_ATT]
    v = proj_a[:, 2 * D_ATT:3 * D_ATT]

    grp = grp_ref[...]

    def head_norm(t, g_ref):
        tt = (t * t).astype(bf16)
        halves = [jnp.dot(tt[:, c * MXU_DIM:(c + 1) * MXU_DIM], grp, preferred_element_type=f32)
                  for c in range(D_ATT // MXU_DIM)]
        msq = jnp.concatenate(halves, axis=1)
        return t * lax.rsqrt(msq + EPS) * g_ref[...]

    qn = head_norm(q, gq_ref)
    kn = head_norm(k, gk_ref).astype(bf16)

    lane = lax.broadcasted_iota(jnp.int32, (tm, FG_LANES), 1)
    lf = jax.nn.log_sigmoid(fg + bf_ref[...]) * LOG2E
    l0, l1, l2 = _split3(lf)
    packed = _pack_groups(lane, l0, l1, l2, 0.0).astype(bf16)
    part = jnp.dot(tri_ref[...], packed, preferred_element_type=f32)
    g = ATT_HEADS
    tot = (part + pltpu.roll(part, g, 1) + pltpu.roll(part, 2 * g, 1)
           + pltpu.roll(part, FG_LANES - g, 1) + pltpu.roll(part, FG_LANES - 2 * g, 1))
    cum = tot + carry_ref[0:1, :]
    carry_ref[...] = jnp.broadcast_to(cum[tm - 1:tm, :], carry_ref.shape)

    c0, c1, c2 = _split3(cum)
    ones_lane = jnp.where(lane == _SPLITS * ATT_HEADS, 1.0, 0.0)
    aug_f32 = _pack_groups(lane, c0, c1, c2, ones_lane)
    aug_src = aug_f32.astype(bf16)
    ka = jnp.dot(aug_src, ek_ref[...], preferred_element_type=f32).astype(bf16)
    qat = jnp.dot(eqt_ref[...], aug_f32.T.astype(bf16),
                  preferred_element_type=f32).astype(bf16)

    pieces = []
    for p in range(N_PAIRS):
        sl = slice(p * LANES, (p + 1) * LANES)
        pieces += [kn[:, sl], ka[:, sl]]
    kk_ref[...] = jnp.concatenate(pieces, axis=1)

    qnt = qn.T.astype(bf16)
    vt = v.T.astype(bf16)
    for p in range(N_PAIRS):
        sl = slice(p * LANES, (p + 1) * LANES)
        qt_ref[0, p, 0] = jnp.concatenate([qnt[sl, :], qat[sl, :]], axis=0)
        vt_ref[0, p, 0] = vt[sl, :]

    conv_ref[...] = _conformer_conv(cv, hbuf, dww_ref, dwb_ref, lng_ref, lnb_ref,
                                    pw_ref).astype(bf16)
    pool_ref[...] = _multiscale_pool(pin, ubuf, poolw_ref, pscale_ref, t_in_seq).astype(bf16)


def _in_proj_mix(x2, g1, w_cat, bf_pad, gq, gk, consts, mix_params, batch, seq):
    t = x2.shape[0]
    tm = ROW_TILE
    tiles_per_seq = seq // tm
    n_tiles = t // tm
    const = lambda shape: pl.BlockSpec(shape, lambda i: (0,) * len(shape))
    return pl.pallas_call(
        functools.partial(_in_kernel, tiles_per_seq=tiles_per_seq),
        grid=(n_tiles,),
        in_specs=[
            pl.BlockSpec((tm, D_MODEL), lambda i: (i, 0)),
            const((1, D_MODEL)),
            const((D_MODEL, N_PROJ)),
            const((1, FG_LANES)),
            const((1, D_ATT)),
            const((1, D_ATT)),
            const((MXU_DIM, MXU_DIM)),
            const((tm, tm)),
            const((N_PAIRS * LANES, LANES)),
            const((LANES, N_PAIRS * LANES)),
            const((HALO, D_CONV)),
            const((1, D_CONV)),
            const((1, D_CONV)),
            const((1, D_CONV)),
            const((D_CONV, D_CONV)),
            const((D_POOL, D_POOL)),
            const((1, D_POOL)),
        ],
        out_specs=[
            pl.BlockSpec((1, N_PAIRS, 1, 2 * LANES, tm),
                         lambda i: (i // tiles_per_seq, 0, i % tiles_per_seq, 0, 0)),
            pl.BlockSpec((tm, 2 * D_ATT), lambda i: (i, 0)),
            pl.BlockSpec((1, N_PAIRS, 1, LANES, tm),
                         lambda i: (i // tiles_per_seq, 0, i % tiles_per_seq, 0, 0)),
            pl.BlockSpec((tm, D_CONV), lambda i: (i, 0)),
            pl.BlockSpec((tm, D_POOL), lambda i: (i, 0)),
        ],
        out_shape=[
            jax.ShapeDtypeStruct((batch, N_PAIRS, tiles_per_seq, 2 * LANES, tm), bf16),
            jax.ShapeDtypeStruct((t, 2 * D_ATT), bf16),
            jax.ShapeDtypeStruct((batch, N_PAIRS, tiles_per_seq, LANES, tm), bf16),
            jax.ShapeDtypeStruct((t, D_CONV), bf16),
            jax.ShapeDtypeStruct((t, D_POOL), bf16),
        ],
        scratch_shapes=[pltpu.VMEM((SUBLANES, FG_LANES), f32),
                        pltpu.VMEM((HALO + tm, D_CONV), f32),
                        pltpu.VMEM((HALO + tm, D_POOL), f32)],
        compiler_params=pltpu.CompilerParams(
            dimension_semantics=("arbitrary",), vmem_limit_bytes=VMEM_LIMIT),
        name="in_proj_mix",
    )(x2, g1, w_cat, bf_pad, gq, gk, *consts, *mix_params)


def _block_schedule(nq):
    blocks = [(i, j) for i in range(nq) for j in range(i)]
    n_unmasked = len(blocks)
    blocks += [(i, i) for i in range(nq)]
    bi = np.asarray([b[0] for b in blocks], np.int32)
    bj = np.asarray([b[1] for b in blocks], np.int32)
    return bi, bj, n_unmasked


def _att_kernel(bi_ref, bj_ref, qt_ref, kk_ref, vt_ref, o_ref, qh_ref, s0_ref, s1_ref,
                mt0_ref, mt1_ref, m_ref, acc_ref, *, n_unmasked):
    nq, _, _, tq = qh_ref.shape
    n_blocks = n_unmasked + nq
    feat = lax.broadcasted_iota(jnp.int32, (2 * LANES, 1), 0)
    aug = feat - LANES
    row = lax.broadcasted_iota(jnp.int32, (LANES, 1), 0)

    def in_range(v, lo, hi):
        return (v >= lo) & (v < hi)

    s = _SPLITS
    keep = [
        in_range(feat, 0, HEAD_DIM) | in_range(aug, 0, s) | in_range(aug, 2 * s, 3 * s),
        in_range(feat, HEAD_DIM, LANES) | in_range(aug, s, 2 * s) | in_range(aug, 3 * s, 4 * s),
    ]
    for it in range(nq):
        qt = qt_ref[0, 0, it]
        for hh in range(2):
            qh_ref[it, hh] = jnp.where(keep[hh], qt, jnp.zeros_like(qt))
    vkeep = [row < HEAD_DIM, row >= HEAD_DIM]
    m_ref[...] = jnp.full(m_ref.shape, NEG, f32)
    acc_ref[...] = jnp.zeros(acc_ref.shape, f32)
    s_refs = (s0_ref, s1_ref)
    mt_refs = (mt0_ref, mt1_ref)

    hq = tq // 2

    def produce(slot, n, hh, diag):
        i = bi_ref[n]
        off = pl.multiple_of(bj_ref[n] * tq, tq)
        if not diag:
            kt = kk_ref[0, pl.ds(off, tq), :]
            st = jnp.dot(kt, qh_ref[i, hh], preferred_element_type=f32)
            s_refs[slot][hh] = st
            mt_refs[slot][hh] = jnp.max(st, axis=0, keepdims=True)
        else:
            k_early = kk_ref[0, pl.ds(off, hq), :]
            k_late = kk_ref[0, pl.ds(pl.multiple_of(off + hq, hq), hq), :]
            s_refs[slot][hh, 0:hq, :] = jnp.dot(k_early, qh_ref[i, hh],
                                                preferred_element_type=f32)
            s_refs[slot][hh, hq:tq, hq:tq] = jnp.dot(k_late, qh_ref[i, hh, :, hq:tq],
                                                     preferred_element_type=f32)

    def update(i, hh, cols, st, mt, vh):
        m = m_ref[i, hh, :, cols]
        m_new = jnp.maximum(m, mt)
        p = jnp.exp2(st - m_new).astype(bf16)
        alpha = jnp.exp2(m - m_new)
        acc_ref[i, hh, :, cols] = (acc_ref[i, hh, :, cols] * alpha
                                   + jnp.dot(vh, p, preferred_element_type=f32))
        m_ref[i, hh, :, cols] = m_new

    def consume(slot, n, hh, diag):
        i = bi_ref[n]
        vt = vt_ref[0, 0, bj_ref[n]]
        vh = jnp.where(vkeep[hh], vt, jnp.ones_like(vt))
        if not diag:
            update(i, hh, slice(0, tq), s_refs[slot][hh], mt_refs[slot][hh], vh)
        else:
            for cols, nk in ((slice(0, hq), hq), (slice(hq, tq), tq)):
                st = s_refs[slot][hh, 0:nk, cols]
                kidx = lax.broadcasted_iota(jnp.int32, st.shape, 0)
                qidx = lax.broadcasted_iota(jnp.int32, st.shape, 1) + cols.start
                st = jnp.where(kidx <= qidx, st, NEG)
                update(i, hh, cols, st, jnp.max(st, axis=0, keepdims=True), vh[:, 0:nk])

    def step(n, slot, last=False):
        static = isinstance(n, int)
        diag = static and n >= n_unmasked
        next_diag = static and n + 1 >= n_unmasked
        for hh in range(2):
            if not last:
                produce(1 - slot, n + 1, hh, next_diag)
            consume(slot, n, hh, diag)

    for hh in range(2):
        produce(0, 0, hh, n_unmasked == 0)
    looped = (max(n_unmasked - 1, 0) // ATT_UNROLL) * ATT_UNROLL
    if looped:
        def body(t, c):
            for u in range(ATT_UNROLL):
                step(t * ATT_UNROLL + u, u % 2)
            return c
        lax.fori_loop(0, looped // ATT_UNROLL, body, 0)
    for n in range(looped, n_blocks):
        step(n, n % 2, last=(n == n_blocks - 1))

    for it in range(nq):
        acc0 = acc_ref[it, 0]
        acc1 = acc_ref[it, 1]
        out0 = acc0 / acc0[HEAD_DIM:HEAD_DIM + 1, :]
        out1 = acc1 / acc1[0:1, :]
        ot = jnp.where(row < HEAD_DIM, out0, out1)
        o_ref[0, it * tq:(it + 1) * tq, :] = ot.T.astype(bf16)


def _attention(qq, kk, vt, batch, seq):
    tq = ROW_TILE
    nq = seq // tq
    bi, bj, n_unmasked = _block_schedule(nq)
    grid_spec = pltpu.PrefetchScalarGridSpec(
        num_scalar_prefetch=2,
        grid=(batch, N_PAIRS),
        in_specs=[
            pl.BlockSpec((1, 1, nq, 2 * LANES, tq), lambda b, p, bi, bj: (b, p, 0, 0, 0)),
            pl.BlockSpec((1, seq, 2 * LANES), lambda b, p, bi, bj: (b, 0, p)),
            pl.BlockSpec((1, 1, nq, LANES, tq), lambda b, p, bi, bj: (b, p, 0, 0, 0)),
        ],
        out_specs=pl.BlockSpec((1, seq, LANES), lambda b, p, bi, bj: (b, 0, p)),
        scratch_shapes=[pltpu.VMEM((nq, 2, 2 * LANES, tq), bf16),
                        pltpu.VMEM((2, tq, tq), f32), pltpu.VMEM((2, tq, tq), f32),
                        pltpu.VMEM((2, 1, tq), f32), pltpu.VMEM((2, 1, tq), f32),
                        pltpu.VMEM((nq, 2, 1, tq), f32), pltpu.VMEM((nq, 2, LANES, tq), f32)],
    )
    return pl.pallas_call(
        functools.partial(_att_kernel, n_unmasked=n_unmasked),
        grid_spec=grid_spec,
        out_shape=jax.ShapeDtypeStruct((batch, seq, D_ATT), bf16),
        compiler_params=pltpu.CompilerParams(
            dimension_semantics=("arbitrary", "arbitrary"), vmem_limit_bytes=VMEM_LIMIT),
        name="fox_attention",
    )(jnp.asarray(bi), jnp.asarray(bj), qq, kk, vt)


def _ffn_kernel(x_ref, att_ref, conv_ref, pool_ref, wo_ref, g2_ref, wup_ref, dw_ref, wdn_ref,
                out_ref, tail_ref, act_ref, *, tiles_per_seq):
    tm = x_ref.shape[0]

    @pl.when(pl.program_id(0) % tiles_per_seq == 0)
    def _():
        tail_ref[...] = jnp.zeros_like(tail_ref)

    mix = jnp.concatenate([att_ref[...], conv_ref[...], pool_ref[...]], axis=1)
    x1 = x_ref[...] + jnp.dot(mix, wo_ref[...], preferred_element_type=f32)
    ms = jnp.mean(x1 * x1, axis=-1, keepdims=True)
    h2 = (x1 * lax.rsqrt(ms + EPS) * g2_ref[...]).astype(bf16)

    def up_conv(col):
        cols = slice(col, col + FF_CHUNK)
        up = jnp.dot(h2, wup_ref[:, cols], preferred_element_type=f32)
        ext = jnp.concatenate([tail_ref[:, cols], up], axis=0)
        tail_ref[:, cols] = up[tm - SUBLANES:tm, :]
        s1 = pltpu.roll(ext, 1, 0)[SUBLANES:, :]
        s2 = pltpu.roll(ext, 2, 0)[SUBLANES:, :]
        w = dw_ref[:, cols]
        return w[0:1, :] * s2 + w[1:2, :] * s1 + w[2:3, :] * up

    for c in range(N_FF_CHUNKS):
        gate = up_conv(c * FF_CHUNK)
        val = up_conv(D_FF + c * FF_CHUNK)
        act_ref[:, c * FF_CHUNK:(c + 1) * FF_CHUNK] = (
            gate * jax.nn.sigmoid(gate) * val).astype(bf16)

    out_ref[...] = x1 + jnp.dot(act_ref[...], wdn_ref[...], preferred_element_type=f32)


def _ffn(x2, att, conv, pool, wo, g2, wup, dw, wdn, layer, seq):
    t = x2.shape[0]
    tm = FFN_TILE
    tiles_per_seq = seq // tm
    resident = lambda shape: pl.BlockSpec(shape, lambda i: (0,) * len(shape),
                                          pipeline_mode=pl.Buffered(1))
    of_layer = lambda shape: pl.BlockSpec((None,) + shape, lambda i: (layer, 0, 0),
                                          pipeline_mode=pl.Buffered(1))
    return pl.pallas_call(
        functools.partial(_ffn_kernel, tiles_per_seq=tiles_per_seq),
        grid=(t // tm,),
        in_specs=[
            pl.BlockSpec((tm, D_MODEL), lambda i: (i, 0)),
            pl.BlockSpec((tm, D_ATT), lambda i: (i, 0)),
            pl.BlockSpec((tm, D_CONV), lambda i: (i, 0)),
            pl.BlockSpec((tm, D_POOL), lambda i: (i, 0)),
            of_layer((D_MODEL, D_MODEL)),
            resident((1, D_MODEL)),
            of_layer((D_MODEL, 2 * D_FF)),
            resident((SUBLANES, 2 * D_FF)),
            of_layer((D_FF, D_MODEL)),
        ],
        out_specs=pl.BlockSpec((tm, D_MODEL), lambda i: (i, 0)),
        out_shape=jax.ShapeDtypeStruct((t, D_MODEL), f32),
        scratch_shapes=[pltpu.VMEM((SUBLANES, 2 * D_FF), f32),
                        pltpu.VMEM((tm, D_FF), bf16)],
        compiler_params=pltpu.CompilerParams(
            dimension_semantics=("arbitrary",), vmem_limit_bytes=VMEM_LIMIT),
        name="out_proj_ffn",
    )(x2, att, conv, pool, wo, g2, wup, dw, wdn)


def _constants():
    grp = np.kron(np.eye(MXU_DIM // HEAD_DIM, dtype=np.float32),
                  np.full((HEAD_DIM, HEAD_DIM), 1.0 / HEAD_DIM, np.float32))
    tri = np.tril(np.ones((ROW_TILE, ROW_TILE), np.float32))
    eq, ek = _placement_matrices()
    return tuple(jnp.asarray(a, bf16) for a in (grp, tri, eq.T, ek))


def kernel(x, norm1_g, w_in, b_f, q_norm_g, k_norm_g, conv_dw_w, conv_dw_b, conv_ln_g, conv_ln_b,
           conv_pw_w, pool_w, pool_scale, w_out, norm2_g, w_up, ffn_dw_w, w_down):
    batch, seq, _ = x.shape
    depth = w_in.shape[0]
    assert seq % ROW_TILE == 0
    consts = _constants()
    x2 = x.reshape(batch * seq, D_MODEL)
    w_out_b, w_up_b, w_down_b = (w.astype(bf16) for w in (w_out, w_up, w_down))

    for l in range(depth):
        wi = w_in[l]
        o = 3 * D_ATT
        w_fg = wi[:, o:o + ATT_HEADS]
        o += ATT_HEADS
        w_cv = wi[:, o:o + 2 * D_CONV]
        o += 2 * D_CONV
        w_pl = wi[:, o:o + D_POOL]
        fg_pad = jnp.zeros((D_MODEL, FG_LANES - _SPLITS * ATT_HEADS), f32)
        w_cat = jnp.concatenate([w_cv, w_pl] + [w_fg] * _SPLITS + [fg_pad, wi[:, :3 * D_ATT]],
                                axis=1).astype(bf16)
        bf_pad = jnp.concatenate([b_f[l]] * _SPLITS
                                 + [jnp.zeros((FG_LANES - _SPLITS * ATT_HEADS,), f32)])[None, :]
        gq = jnp.tile(q_norm_g[l] * (ATT_SCALE * LOG2E), ATT_HEADS)[None, :]
        gk = jnp.tile(k_norm_g[l], ATT_HEADS)[None, :]
        dww = jnp.concatenate([conv_dw_w[l], jnp.zeros((HALO - CONV_WIDTH, D_CONV), f32)], axis=0)
        poolw = jax.scipy.linalg.block_diag(*[pool_w[l, g] for g in range(len(POOL_WINDOWS))])
        mix_params = (dww, conv_dw_b[l][None, :], conv_ln_g[l][None, :], conv_ln_b[l][None, :],
                      conv_pw_w[l].astype(bf16), poolw.astype(bf16), pool_scale[l][None, :])

        qt, kk, vt, conv, pool = _in_proj_mix(x2, norm1_g[l][None, :], w_cat, bf_pad, gq, gk,
                                              consts, mix_params, batch, seq)

        att = _attention(qt, kk.reshape(batch, seq, 2 * D_ATT),
                         vt, batch, seq).reshape(batch * seq, D_ATT)

        dw = jnp.concatenate(
            [ffn_dw_w[l], jnp.zeros((SUBLANES - FFN_CONV_WIDTH, 2 * D_FF), f32)], axis=0)
        x2 = _ffn(x2, att, conv, pool, w_out_b, norm2_g[l][None, :], w_up_b, dw, w_down_b, l, seq)

    return x2.reshape(batch, seq, D_MODEL)
```
